```python
import math
import jax, jax.numpy as jnp
from jax import lax
import numpy as np

D_MODEL = 2048
BATCH = 2
SEQ = 8192
DEPTH = 4
DEC_BATCH = 8
DEC_SEQ = 64
PAST_LEN = 1024

CHUNK = 64
Q_BLOCK = 128
N_HEADS = 8
HEAD_DIM = 128
V_DIM = 2 * HEAD_DIM
ATT_WIDTH = N_HEADS * V_DIM
QK_WIDTH = N_HEADS * 2 * HEAD_DIM
ROT_DIM = HEAD_DIM // 4
ROPE_THETA = 500000.0
GMLP_CHUNK = 128
GMLP_GROUPS = 8
GMLP_WIDTH = 2048
GMLP_GROUP_DIM = GMLP_WIDTH // GMLP_GROUPS
N_BRANCH = 2
IN_WIDTH = 2 * QK_WIDTH + ATT_WIDTH + 2 * GMLP_WIDTH + N_BRANCH * D_MODEL
N_EXPERTS = 32
TOP_K = 4
D_FF = D_MODEL
SWIGLU_LIMIT = 7.0
SWIGLU_ALPHA = 1.702
EPS = 1e-6
NEG = -1e30
ATT_SCALE = 1.0 / math.sqrt(HEAD_DIM)

kernel_name = "diffattn_gmlp_moe_streaming_step"


def rmsnorm(x, g):
    xf = x.astype(jnp.float32)
    y = xf * lax.rsqrt(jnp.mean(xf * xf, axis=-1, keepdims=True) + EPS) * g.astype(jnp.float32)
    return y.astype(x.dtype)


def layernorm(x, g, b):
    xf = x.astype(jnp.float32)
    mu = jnp.mean(xf, axis=-1, keepdims=True)
    var = jnp.mean(jnp.square(xf - mu), axis=-1, keepdims=True)
    y = (xf - mu) * lax.rsqrt(var + EPS) * g.astype(jnp.float32) + b.astype(jnp.float32)
    return y.astype(x.dtype)


def apply_rope(x, pos):
    half = ROT_DIM // 2
    freqs = ROPE_THETA ** (-jnp.arange(0, ROT_DIM, 2, dtype=jnp.float32) / ROT_DIM)
    ang = pos.astype(jnp.float32)[:, None] * freqs[None, :]
    cos = jnp.cos(ang)[None, :, None, None, :]
    sin = jnp.sin(ang)[None, :, None, None, :]
    xf = x.astype(jnp.float32)
    x1 = xf[..., :half]
    x2 = xf[..., half:ROT_DIM]
    out = jnp.concatenate([x1 * cos - x2 * sin, x2 * cos + x1 * sin, xf[..., ROT_DIM:]], axis=-1)
    return out.astype(x.dtype)


def project_in(h, w_in_l, qn_g, kn_g, pos):
    B, T = h.shape[0], h.shape[1]
    z = h @ w_in_l
    o1 = QK_WIDTH
    o2 = o1 + QK_WIDTH
    o3 = o2 + ATT_WIDTH
    o4 = o3 + GMLP_WIDTH
    o5 = o4 + GMLP_WIDTH
    q = z[..., :o1].reshape(B, T, N_HEADS, 2, HEAD_DIM)
    k = z[..., o1:o2].reshape(B, T, N_HEADS, 2, HEAD_DIM)
    v = z[..., o2:o3].reshape(B, T, N_HEADS, V_DIM)
    u = z[..., o3:o4]
    vg = z[..., o4:o5]
    gates = z[..., o5:].reshape(B, T, N_BRANCH, D_MODEL)
    q = apply_rope(rmsnorm(q, qn_g), pos)
    k = apply_rope(rmsnorm(k, kn_g), pos)
    return q, k, v, u, vg, gates


def diff_attend(q, k, v, lam, mask):
    s = jnp.einsum('bqhmd,bkhmd->bhmqk', q, k).astype(jnp.float32) * ATT_SCALE
    if mask is not None:
        s = jnp.where(mask, s, NEG)
    p = jax.nn.softmax(s, axis=-1)
    a = p[:, :, 0] - lam * p[:, :, 1]
    return jnp.einsum('bhqk,bkhd->bqhd', a, v.astype(jnp.float32))


def attend_prompt(q, k, v, lam):
    B, T = q.shape[0], q.shape[1]
    nb = T // Q_BLOCK
    qb = jnp.moveaxis(q.reshape(B, nb, Q_BLOCK, N_HEADS, 2, HEAD_DIM), 1, 0)
    k_chunk = jnp.arange(T) // CHUNK

    def block(args):
        q_blk, i = args
        q_chunk = (i * Q_BLOCK + jnp.arange(Q_BLOCK)) // CHUNK
        mask = k_chunk[None, :] <= q_chunk[:, None]
        return diff_attend(q_blk, k, v, lam, mask)

    o = lax.map(block, (qb, jnp.arange(nb)))
    return jnp.moveaxis(o, 0, 1).reshape(B, T, N_HEADS, V_DIM)


def gmlp_prepare(u, vg, ln_g, ln_b):
    u = jax.nn.gelu(u)
    vg = layernorm(jax.nn.gelu(vg), ln_g, ln_b)
    return u, vg


def gmlp_mix_prompt(u, vg, w_s, b_s):
    B, T = u.shape[0], u.shape[1]
    nc = T // GMLP_CHUNK
    tri = jnp.tril(jnp.ones((GMLP_CHUNK, GMLP_CHUNK), dtype=w_s.dtype))
    w = w_s * tri[None]
    vr = vg.reshape(B, nc, GMLP_CHUNK, GMLP_GROUPS, GMLP_GROUP_DIM)
    mixed = jnp.einsum('gts,bnsgc->bntgc', w, vr) + b_s.T[None, None, :, :, None]
    return u * mixed.reshape(B, T, GMLP_WIDTH)


def gmlp_mix_sample(u, vg, w_s, b_s):
    B, S = u.shape[0], u.shape[1]
    tri = jnp.tril(jnp.ones((S, S), dtype=w_s.dtype))
    w = w_s[:, :S, :S] * tri[None]
    vr = vg.reshape(B, S, GMLP_GROUPS, GMLP_GROUP_DIM)
    mixed = jnp.einsum('gts,bsgc->btgc', w, vr) + b_s[:, :S].T[None, :, :, None]
    return u * mixed.reshape(B, S, GMLP_WIDTH)


def merge_branches(att_o, gm_o, gates, gate_b, w_pa, w_pb, w_out):
    g = jax.nn.sigmoid((gates + gate_b.reshape(N_BRANCH, D_MODEL)).astype(jnp.float32)).astype(att_o.dtype)
    merged = g[..., 0, :] * (att_o @ w_pa) + g[..., 1, :] * (gm_o @ w_pb)
    return merged @ w_out


def moe(h, w_router, b_router, w_gu, b_gu, w_dn, b_dn):
    shp = h.shape
    hf = h.reshape(-1, D_MODEL)
    logits = (hf @ w_router + b_router).astype(jnp.float32)
    top_v, top_i = lax.top_k(logits, TOP_K)
    top_w = jax.nn.softmax(top_v, axis=-1)
    combine = jnp.einsum('tk,tke->te', top_w, jax.nn.one_hot(top_i, N_EXPERTS, dtype=jnp.float32))
    y = jnp.zeros(hf.shape, jnp.float32)
    for e in range(N_EXPERTS):
        gu = hf @ w_gu[e] + b_gu[e]
        gate = jnp.minimum(gu[:, :D_FF], SWIGLU_LIMIT)
        up = jnp.clip(gu[:, D_FF:], -SWIGLU_LIMIT, SWIGLU_LIMIT)
        act = (up + 1.0) * (gate * jax.nn.sigmoid(SWIGLU_ALPHA * gate))
        y = y + combine[:, e:e + 1] * (act @ w_dn[e] + b_dn[e]).astype(jnp.float32)
    return y.astype(h.dtype).reshape(shp)


def setup_inputs(seed: int = 0) -> dict:
    key = jax.random.key(seed)
    ks = jax.random.split(key, 32)
    f32 = jnp.float32
    nrm = lambda k, s, sc: jax.random.normal(k, s, f32) * sc
    return {
        "x_prompt": nrm(ks[0], (BATCH, SEQ, D_MODEL), 1.0),
        "x_sample": nrm(ks[1], (DEC_BATCH, DEC_SEQ, D_MODEL), 1.0),
        "cache_k": nrm(ks[2], (DEPTH, DEC_BATCH, PAST_LEN, N_HEADS, 2, HEAD_DIM), 1.0),
        "cache_v": nrm(ks[3], (DEPTH, DEC_BATCH, PAST_LEN, N_HEADS, V_DIM), 1.0),
        "norm1_g": 1.0 + nrm(ks[4], (DEPTH, D_MODEL), 0.02),
        "norm2_g": 1.0 + nrm(ks[5], (DEPTH, D_MODEL), 0.02),
        "w_in": nrm(ks[6], (DEPTH, D_MODEL, IN_WIDTH), D_MODEL ** -0.5),
        "q_norm_g": 1.0 + nrm(ks[7], (DEPTH, HEAD_DIM), 0.02),
        "k_norm_g": 1.0 + nrm(ks[8], (DEPTH, HEAD_DIM), 0.02),
        "lambda_qk": nrm(ks[9], (DEPTH, 4, HEAD_DIM), 0.1),
        "subln_g": 1.0 + nrm(ks[10], (DEPTH, V_DIM), 0.02),
        "gmlp_ln_g": 1.0 + nrm(ks[11], (DEPTH, GMLP_WIDTH), 0.02),
        "gmlp_ln_b": nrm(ks[12], (DEPTH, GMLP_WIDTH), 0.02),
        "gmlp_w_s": nrm(ks[13], (DEPTH, GMLP_GROUPS, GMLP_CHUNK, GMLP_CHUNK), GMLP_CHUNK ** -0.5),
        "gmlp_b_s": 1.0 + nrm(ks[14], (DEPTH, GMLP_GROUPS, GMLP_CHUNK), 0.1),
        "gate_b": nrm(ks[15], (DEPTH, N_BRANCH * D_MODEL), 0.01),
        "w_proj_a": nrm(ks[16], (DEPTH, ATT_WIDTH, D_MODEL), ATT_WIDTH ** -0.5),
        "w_proj_b": nrm(ks[17], (DEPTH, GMLP_WIDTH, D_MODEL), GMLP_WIDTH ** -0.5),
        "w_out": nrm(ks[18], (DEPTH, D_MODEL, D_MODEL), 0.5 * D_MODEL ** -0.5),
        "w_router": nrm(ks[19], (DEPTH, D_MODEL, N_EXPERTS), D_MODEL ** -0.5),
        "b_router": nrm(ks[20], (DEPTH, N_EXPERTS), 0.01),
        "w_gu": nrm(ks[21], (DEPTH, N_EXPERTS, D_MODEL, 2 * D_FF), D_MODEL ** -0.5),
        "b_gu": nrm(ks[22], (DEPTH, N_EXPERTS, 2 * D_FF), 0.01),
        "w_dn": nrm(ks[23], (DEPTH, N_EXPERTS, D_FF, D_MODEL), 0.5 * D_FF ** -0.5),
        "b_dn": nrm(ks[24], (DEPTH, N_EXPERTS, D_MODEL), 0.01),
    }


def reference(x_prompt, x_sample, cache_k, cache_v, norm1_g, norm2_g, w_in, q_norm_g, k_norm_g,
              lambda_qk, subln_g, gmlp_ln_g, gmlp_ln_b, gmlp_w_s, gmlp_b_s, gate_b, w_proj_a,
              w_proj_b, w_out, w_router, b_router, w_gu, b_gu, w_dn, b_dn):
    T = x_prompt.shape[1]
    S = x_sample.shape[1]
    past = cache_k.shape[2]
    pos_p = jnp.arange(T)
    pos_s = past + jnp.arange(S)
    xp, xs = x_prompt, x_sample
    kp_l, vp_l, ks_l, vs_l, gs_l = [], [], [], [], []
    for l in range(DEPTH):
        lam_init = 0.8 - 0.6 * math.exp(-0.3 * l)
        lq = lambda_qk[l].astype(jnp.float32)
        lam = jnp.exp(jnp.sum(lq[0] * lq[1])) - jnp.exp(jnp.sum(lq[2] * lq[3])) + lam_init

        h = rmsnorm(xp, norm1_g[l])
        q, k, v, u, vg, gates = project_in(h, w_in[l], q_norm_g[l], k_norm_g[l], pos_p)
        o = attend_prompt(q, k, v, lam)
        o = (rmsnorm(o, subln_g[l]) * (1.0 - lam_init)).astype(h.dtype).reshape(xp.shape[0], T, ATT_WIDTH)
        u, vg = gmlp_prepare(u, vg, gmlp_ln_g[l], gmlp_ln_b[l])
        gm = gmlp_mix_prompt(u, vg, gmlp_w_s[l], gmlp_b_s[l])
        xp = xp + merge_branches(o, gm, gates, gate_b[l], w_proj_a[l], w_proj_b[l], w_out[l])
        kp_l.append(k)
        vp_l.append(v)

        h = rmsnorm(xs, norm1_g[l])
        q, k, v, u, vg, gates = project_in(h, w_in[l], q_norm_g[l], k_norm_g[l], pos_s)
        k_all = jnp.concatenate([cache_k[l].astype(k.dtype), k], axis=1)
        v_all = jnp.concatenate([cache_v[l].astype(v.dtype), v], axis=1)
        o = diff_attend(q, k_all, v_all, lam, None)
        o = (rmsnorm(o, subln_g[l]) * (1.0 - lam_init)).astype(h.dtype).reshape(xs.shape[0], S, ATT_WIDTH)
        u, vg = gmlp_prepare(u, vg, gmlp_ln_g[l], gmlp_ln_b[l])
        gm = gmlp_mix_sample(u, vg, gmlp_w_s[l], gmlp_b_s[l])
        xs = xs + merge_branches(o, gm, gates, gate_b[l], w_proj_a[l], w_proj_b[l], w_out[l])
        ks_l.append(k)
        vs_l.append(v)
        gs_l.append(vg)

        xp = xp + moe(rmsnorm(xp, norm2_g[l]), w_router[l], b_router[l], w_gu[l], b_gu[l], w_dn[l], b_dn[l])
        xs = xs + moe(rmsnorm(xs, norm2_g[l]), w_router[l], b_router[l], w_gu[l], b_gu[l], w_dn[l], b_dn[l])

    new_k_prompt = jnp.stack(kp_l, axis=0)
    new_v_prompt = jnp.stack(vp_l, axis=0)
    new_k_sample = jnp.stack(ks_l, axis=0)
    new_v_sample = jnp.stack(vs_l, axis=0)
    new_gmlp_v_sample = jnp.stack(gs_l, axis=0)
    return (xp, xs, new_k_prompt, new_v_prompt, new_k_sample, new_v_sample, new_gmlp_v_sample)
```

```python
import functools
import math

import jax
import jax.numpy as jnp
from jax import lax
from jax.experimental import pallas as pl
from jax.experimental.pallas import tpu as pltpu

F32 = jnp.float32
BF16 = jnp.bfloat16

HEAD_DIM = 128
V_DIM = 2 * HEAD_DIM
CHUNK = 64
GMLP_CHUNK = 128
ROT_DIM = HEAD_DIM // 4
ROT_HALF = ROT_DIM // 2
ROPE_THETA = 500000.0
TOP_K = 4
SWIGLU_LIMIT = 7.0
SWIGLU_ALPHA = 1.702
EPS = 1e-6
NEG = -1e30
ATT_SCALE = 1.0 / math.sqrt(HEAD_DIM)
GELU_C = math.sqrt(2.0 / math.pi)

VMEM_LIMIT = 48 * 1024 * 1024
NT_DIMS = (((1,), (1,)), ((), ()))


def _params(*sem):
    return pltpu.CompilerParams(dimension_semantics=sem, vmem_limit_bytes=VMEM_LIMIT)


def _gelu(x):
    return 0.5 * x * (1.0 + jnp.tanh(GELU_C * (x + 0.044715 * (x * x * x))))


def _rms(x, g):
    return x * lax.rsqrt(jnp.mean(x * x, axis=-1, keepdims=True) + EPS) * g


def _nm_kernel(*refs, epilogue, n_extra, n_out):
    x_ref, g_ref, w_ref = refs[:3]
    extra = refs[3:3 + n_extra]
    outs = refs[3 + n_extra:3 + n_extra + n_out]
    h_ref = refs[-1]
    j = pl.program_id(1)

    @pl.when(j == 0)
    def _():
        h_ref[...] = _rms(x_ref[...], g_ref[...]).astype(BF16)

    z = jnp.dot(h_ref[...], w_ref[...], preferred_element_type=F32)
    epilogue(z, j, extra, outs)


def _norm_matmul(x, g, w, col0, ncols, tm, tn, epilogue, extras, extra_specs,
                 out_shapes, out_specs):
    t, d = x.shape
    assert t % tm == 0 and ncols % tn == 0 and col0 % tn == 0
    jb = col0 // tn
    in_specs = [
        pl.BlockSpec((tm, d), lambda i, j: (i, 0)),
        pl.BlockSpec((1, d), lambda i, j: (0, 0)),
        pl.BlockSpec((d, tn), lambda i, j: (0, jb + j)),
    ] + list(extra_specs)
    kern = functools.partial(_nm_kernel, epilogue=epilogue, n_extra=len(extras),
                             n_out=len(out_shapes))
    return pl.pallas_call(
        kern,
        grid=(t // tm, ncols // tn),
        in_specs=in_specs,
        out_specs=out_specs,
        out_shape=out_shapes,
        scratch_shapes=[pltpu.VMEM((tm, d), BF16)],
        compiler_params=_params("parallel", "arbitrary"),
    )(x, g.reshape(1, d), w, *extras)


def _epi_headnorm_rope(z, j, extra, outs, *, tn, scale):
    hg = extra[0][...]
    c = extra[1][...]
    s1 = extra[2][...]
    s2 = extra[3][...]
    for h in range(tn // HEAD_DIM):
        sl = slice(h * HEAD_DIM, (h + 1) * HEAD_DIM)
        y = _rms(z[:, sl], hg)
        r = (y * c + pltpu.roll(y, HEAD_DIM - ROT_HALF, 1) * s1
             + pltpu.roll(y, ROT_HALF, 1) * s2)
        if scale != 1.0:
            r = r * scale
        for o in outs:
            o[:, sl] = r.astype(o.dtype)


def _epi_copy(z, j, extra, outs):
    for o in outs:
        o[...] = z.astype(o.dtype)


def _epi_gelu(z, j, extra, outs):
    outs[0][...] = _gelu(z).astype(outs[0].dtype)


def _epi_gelu_layernorm(z, j, extra, outs, *, tn, n_tiles):
    o = outs[0]
    a = _gelu(z)
    for jj in range(n_tiles):
        @pl.when(j == jj)
        def _(jj=jj):
            o[:, jj * tn:(jj + 1) * tn] = a

    @pl.when(j == n_tiles - 1)
    def _():
        full = o[...]
        mu = jnp.mean(full, axis=-1, keepdims=True)
        cen = full - mu
        var = jnp.mean(cen * cen, axis=-1, keepdims=True)
        o[...] = cen * lax.rsqrt(var + EPS) * extra[0][...] + extra[1][...]


def _epi_sigmoid_bias(z, j, extra, outs):
    outs[0][...] = jax.nn.sigmoid(z + extra[0][...]).astype(outs[0].dtype)


def _project_in(x, g, w_in, qg, kg, rope, ln_g, ln_b, gate_b, dims, tm, tn):
    t, d = x.shape
    qk, gw = dims
    c_tab, s1_tab, s2_tab = rope
    o1, o2, o3, o4, o5 = qk, 2 * qk, 3 * qk, 3 * qk + gw, 3 * qk + 2 * gw
    row_tile = lambda n: pl.BlockSpec((tm, n), lambda i, j: (i, 0))
    vec = lambda n: pl.BlockSpec((1, n), lambda i, j: (0, 0))
    out_tile = pl.BlockSpec((tm, tn), lambda i, j: (i, j))
    rope_specs = [vec(HEAD_DIM)] + [row_tile(HEAD_DIM)] * 3

    (q_bf,) = _norm_matmul(
        x, g, w_in, 0, qk, tm, tn,
        functools.partial(_epi_headnorm_rope, tn=tn, scale=ATT_SCALE),
        [qg.reshape(1, HEAD_DIM), c_tab, s1_tab, s2_tab], rope_specs,
        [jax.ShapeDtypeStruct((t, qk), BF16)], [out_tile])
    k_f32, k_bf = _norm_matmul(
        x, g, w_in, o1, qk, tm, tn,
        functools.partial(_epi_headnorm_rope, tn=tn, scale=1.0),
        [kg.reshape(1, HEAD_DIM), c_tab, s1_tab, s2_tab], rope_specs,
        [jax.ShapeDtypeStruct((t, qk), F32), jax.ShapeDtypeStruct((t, qk), BF16)],
        [out_tile, out_tile])
    v_f32, v_bf = _norm_matmul(
        x, g, w_in, o2, qk, tm, tn, _epi_copy, [], [],
        [jax.ShapeDtypeStruct((t, qk), F32), jax.ShapeDtypeStruct((t, qk), BF16)],
        [out_tile, out_tile])
    (u_act,) = _norm_matmul(
        x, g, w_in, o3, gw, tm, tn, _epi_gelu, [], [],
        [jax.ShapeDtypeStruct((t, gw), BF16)], [out_tile])
    (vg_n,) = _norm_matmul(
        x, g, w_in, o4, gw, tm, tn,
        functools.partial(_epi_gelu_layernorm, tn=tn, n_tiles=gw // tn),
        [ln_g.reshape(1, gw), ln_b.reshape(1, gw)], [vec(gw), vec(gw)],
        [jax.ShapeDtypeStruct((t, gw), F32)], [row_tile(gw)])
    (gates,) = _norm_matmul(
        x, g, w_in, o5, 2 * d, tm, tn, _epi_sigmoid_bias,
        [gate_b.reshape(1, 2 * d)], [pl.BlockSpec((1, tn), lambda i, j: (0, j))],
        [jax.ShapeDtypeStruct((t, 2 * d), BF16)], [out_tile])
    return q_bf, k_f32, k_bf, v_f32, v_bf, u_act, vg_n, gates


def _lambda(lq_ref, lam_init):
    lq = lq_ref[...]
    a = jnp.sum(lq[0:1] * lq[1:2], axis=-1, keepdims=True)
    b = jnp.sum(lq[2:3] * lq[3:4], axis=-1, keepdims=True)
    return jnp.exp(a) - jnp.exp(b) + lam_init


def _softmax_step(s, v, m, l, acc):
    m_new = jnp.maximum(m, jnp.max(s, axis=-1, keepdims=True))
    alpha = jnp.exp(m - m_new)
    p = jnp.exp(s - m_new)
    l_new = alpha * l + jnp.sum(p, axis=-1, keepdims=True)
    acc_new = alpha * acc + jnp.dot(p.astype(BF16), v, preferred_element_type=F32)
    return m_new, l_new, acc_new


def _attn_finish(lam, g, a1, l1, a2, l2):
    o = a1 / l1 - lam * (a2 / l2)
    return _rms(o, g).astype(BF16)


def _attn_prompt_kernel(lq_ref, g_ref, q_ref, k_ref, v_ref, o_ref, *, tq, lam_init):
    qi = pl.program_id(2)
    q = q_ref[...]
    q1, q2 = q[:, :HEAD_DIM], q[:, HEAD_DIM:]

    def block(kj, carry, masked):
        m1, l1, a1, m2, l2, a2 = carry
        off = pl.multiple_of(kj * tq, tq)
        k = k_ref[pl.ds(off, tq), :]
        v = v_ref[pl.ds(off, tq), :]
        s1 = lax.dot_general(q1, k[:, :HEAD_DIM], NT_DIMS, preferred_element_type=F32)
        s2 = lax.dot_general(q2, k[:, HEAD_DIM:], NT_DIMS, preferred_element_type=F32)
        if masked:
            rc = lax.broadcasted_iota(jnp.int32, (tq, tq), 0) // CHUNK
            cc = lax.broadcasted_iota(jnp.int32, (tq, tq), 1) // CHUNK
            vis = cc <= rc
            s1 = jnp.where(vis, s1, NEG)
            s2 = jnp.where(vis, s2, NEG)
        m1, l1, a1 = _softmax_step(s1, v, m1, l1, a1)
        m2, l2, a2 = _softmax_step(s2, v, m2, l2, a2)
        return m1, l1, a1, m2, l2, a2

    m0 = jnp.full((tq, 1), NEG, F32)
    l0 = jnp.zeros((tq, 1), F32)
    a0 = jnp.zeros((tq, V_DIM), F32)
    carry = lax.fori_loop(0, qi, lambda kj, c: block(kj, c, False),
                          (m0, l0, a0, m0, l0, a0))
    m1, l1, a1, m2, l2, a2 = block(qi, carry, True)
    o_ref[...] = _attn_finish(_lambda(lq_ref, lam_init), g_ref[...], a1, l1, a2, l2)


def _attend_prompt(q_bf, k_bf, v_bf, lq, g_eff, lam_init, batch, seq, tq):
    qk = q_bf.shape[1]
    n_heads = qk // V_DIM
    nq = seq // tq
    kern = functools.partial(_attn_prompt_kernel, tq=tq, lam_init=lam_init)
    return pl.pallas_call(
        kern,
        grid=(batch, n_heads, nq),
        in_specs=[
            pl.BlockSpec((4, HEAD_DIM), lambda b, h, i: (0, 0)),
            pl.BlockSpec((1, V_DIM), lambda b, h, i: (0, 0)),
            pl.BlockSpec((tq, V_DIM), lambda b, h, i: (b * nq + i, h)),
            pl.BlockSpec((seq, V_DIM), lambda b, h, i: (b, h)),
            pl.BlockSpec((seq, V_DIM), lambda b, h, i: (b, h)),
        ],
        out_specs=pl.BlockSpec((tq, V_DIM), lambda b, h, i: (b * nq + i, h)),
        out_shape=jax.ShapeDtypeStruct((batch * seq, qk), BF16),
        compiler_params=_params("parallel", "parallel", "arbitrary"),
    )(lq, g_eff, q_bf, k_bf, v_bf)


def _attn_sample_kernel(lq_ref, g_ref, q_ref, kc_ref, vc_ref, kn_ref, vn_ref, o_ref, *,
                        lam_init):
    q = q_ref[...]
    kc = kc_ref[...].astype(BF16)
    vc = vc_ref[...].astype(BF16)
    kn = kn_ref[...]
    vn = vn_ref[...]
    res = []
    for mp in range(2):
        sl = slice(mp * HEAD_DIM, (mp + 1) * HEAD_DIM)
        sc = lax.dot_general(q[:, sl], kc[:, sl], NT_DIMS, preferred_element_type=F32)
        sn = lax.dot_general(q[:, sl], kn[:, sl], NT_DIMS, preferred_element_type=F32)
        m = jnp.maximum(jnp.max(sc, axis=-1, keepdims=True),
                        jnp.max(sn, axis=-1, keepdims=True))
        pc = jnp.exp(sc - m)
        pn = jnp.exp(sn - m)
        l = jnp.sum(pc, axis=-1, keepdims=True) + jnp.sum(pn, axis=-1, keepdims=True)
        acc = (jnp.dot(pc.astype(BF16), vc, preferred_element_type=F32)
               + jnp.dot(pn.astype(BF16), vn, preferred_element_type=F32))
        res += [acc, l]
    o_ref[...] = _attn_finish(_lambda(lq_ref, lam_init), g_ref[...], *res)


def _attend_sample(q_bf, k_bf, v_bf, cache_k2, cache_v2, layer, lq, g_eff, lam_init,
                   row0, dec_batch, dec_seq, past):
    qk = q_bf.shape[1]
    n_heads = qk // V_DIM
    rb0 = row0 // dec_seq
    new_rows = pl.BlockSpec((dec_seq, V_DIM), lambda b, h: (rb0 + b, h))
    cache_rows = pl.BlockSpec((past, V_DIM), lambda b, h: (layer * dec_batch + b, h))
    kern = functools.partial(_attn_sample_kernel, lam_init=lam_init)
    return pl.pallas_call(
        kern,
        grid=(dec_batch, n_heads),
        in_specs=[
            pl.BlockSpec((4, HEAD_DIM), lambda b, h: (0, 0)),
            pl.BlockSpec((1, V_DIM), lambda b, h: (0, 0)),
            new_rows, cache_rows, cache_rows, new_rows, new_rows,
        ],
        out_specs=pl.BlockSpec((dec_seq, V_DIM), lambda b, h: (b, h)),
        out_shape=jax.ShapeDtypeStruct((dec_batch * dec_seq, qk), BF16),
        compiler_params=_params("parallel", "parallel"),
    )(lq, g_eff, q_bf, cache_k2, cache_v2, k_bf, v_bf)


def _gmlp_kernel(w_ref, b_ref, u_ref, v_ref, o_ref, *, c, n_sub, groups):
    gd = u_ref.shape[1] // groups
    tri = (lax.broadcasted_iota(jnp.int32, (c, c), 0)
           >= lax.broadcasted_iota(jnp.int32, (c, c), 1))
    for g in range(groups):
        wg = jnp.where(tri, w_ref[g], 0.0).astype(BF16)
        bg = b_ref[:, g:g + 1]
        cs = slice(g * gd, (g + 1) * gd)
        for s in range(n_sub):
            rs = slice(s * c, (s + 1) * c)
            mixed = jnp.dot(wg, v_ref[rs, cs].astype(BF16), preferred_element_type=F32) + bg
            o_ref[rs, cs] = (u_ref[rs, cs].astype(F32) * mixed).astype(BF16)


def _gmlp_mix(u, vg, w_s, b_s_t, row0, rows, c, n_sub):
    groups = w_s.shape[0]
    gw = u.shape[1]
    tr = c * n_sub
    rb0 = row0 // tr
    rows_spec = pl.BlockSpec((tr, gw), lambda i: (rb0 + i, 0))
    kern = functools.partial(_gmlp_kernel, c=c, n_sub=n_sub, groups=groups)
    return pl.pallas_call(
        kern,
        grid=(rows // tr,),
        in_specs=[
            pl.BlockSpec((groups, c, c), lambda i: (0, 0, 0)),
            pl.BlockSpec((c, groups), lambda i: (0, 0)),
            rows_spec, rows_spec,
        ],
        out_specs=pl.BlockSpec((tr, gw), lambda i: (i, 0)),
        out_shape=jax.ShapeDtypeStruct((rows, gw), BF16),
        compiler_params=_params("parallel"),
    )(w_s, b_s_t, u, vg)


def _merge_kernel(a_ref, m_ref, wa_ref, wb_ref, g0_ref, g1_ref, o_ref):
    ya = jnp.dot(a_ref[...], wa_ref[...], preferred_element_type=F32)
    yb = jnp.dot(m_ref[...], wb_ref[...], preferred_element_type=F32)
    o_ref[...] = (g0_ref[...].astype(F32) * ya + g1_ref[...].astype(F32) * yb).astype(BF16)


def _merge(att, gm, w_pa, w_pb, gates, tm, tn):
    t, wa = att.shape
    d = w_pa.shape[1]
    nj = d // tn
    return pl.pallas_call(
        _merge_kernel,
        grid=(t // tm, nj),
        in_specs=[
            pl.BlockSpec((tm, wa), lambda i, j: (i, 0)),
            pl.BlockSpec((tm, gm.shape[1]), lambda i, j: (i, 0)),
            pl.BlockSpec((wa, tn), lambda i, j: (0, j)),
            pl.BlockSpec((gm.shape[1], tn), lambda i, j: (0, j)),
            pl.BlockSpec((tm, tn), lambda i, j: (i, j)),
            pl.BlockSpec((tm, tn), lambda i, j: (i, nj + j)),
        ],
        out_specs=pl.BlockSpec((tm, tn), lambda i, j: (i, j)),
        out_shape=jax.ShapeDtypeStruct((t, d), BF16),
        compiler_params=_params("parallel", "arbitrary"),
    )(att, gm, w_pa, w_pb, gates, gates)


def _resid_matmul_kernel(x_ref, m_ref, w_ref, o_ref):
    o_ref[...] = x_ref[...] + jnp.dot(m_ref[...], w_ref[...], preferred_element_type=F32)


def _resid_matmul(x, m, w, tm, tn):
    t, d = x.shape
    return pl.pallas_call(
        _resid_matmul_kernel,
        grid=(t // tm, d // tn),
        in_specs=[
            pl.BlockSpec((tm, tn), lambda i, j: (i, j)),
            pl.BlockSpec((tm, m.shape[1]), lambda i, j: (i, 0)),
            pl.BlockSpec((m.shape[1], tn), lambda i, j: (0, j)),
        ],
        out_specs=pl.BlockSpec((tm, tn), lambda i, j: (i, j)),
        out_shape=jax.ShapeDtypeStruct((t, d), F32),
        compiler_params=_params("parallel", "arbitrary"),
    )(x, m, w)


def _router_kernel(x_ref, g_ref, wr_ref, br_ref, idx_ref, wgt_ref):
    h = _rms(x_ref[...], g_ref[...])
    logits = jnp.dot(h, wr_ref[...], preferred_element_type=F32,
                     precision=lax.Precision.HIGHEST) + br_ref[...]
    n_exp = logits.shape[1]
    lane_e = lax.broadcasted_iota(jnp.int32, logits.shape, 1)
    lane_o = lax.broadcasted_iota(jnp.int32, idx_ref.shape, 1)
    idx_out = jnp.zeros(idx_ref.shape, jnp.int32)
    val_out = jnp.full(wgt_ref.shape, -jnp.inf, F32)
    top = None
    for k in range(TOP_K):
        m = jnp.max(logits, axis=-1, keepdims=True)
        i = jnp.min(jnp.where(logits == m, lane_e, n_exp), axis=-1, keepdims=True)
        logits = jnp.where(lane_e == i, -jnp.inf, logits)
        idx_out = jnp.where(lane_o == k, i, idx_out)
        val_out = jnp.where(lane_o == k, m, val_out)
        if k == 0:
            top = m
    e = jnp.exp(val_out - top)
    idx_ref[...] = idx_out
    wgt_ref[...] = e / jnp.sum(e, axis=-1, keepdims=True)


def _router(x, g, w_router, b_router, tm):
    t, d = x.shape
    n_exp = w_router.shape[1]
    lanes = 128
    return pl.pallas_call(
        _router_kernel,
        grid=(t // tm,),
        in_specs=[
            pl.BlockSpec((tm, d), lambda i: (i, 0)),
            pl.BlockSpec((1, d), lambda i: (0, 0)),
            pl.BlockSpec((d, n_exp), lambda i: (0, 0)),
            pl.BlockSpec((1, n_exp), lambda i: (0, 0)),
        ],
        out_specs=[pl.BlockSpec((tm, lanes), lambda i: (i, 0))] * 2,
        out_shape=[jax.ShapeDtypeStruct((t, lanes), jnp.int32),
                   jax.ShapeDtypeStruct((t, lanes), F32)],
        compiler_params=_params("parallel"),
    )(x, g.reshape(1, d), w_router, b_router.reshape(1, n_exp))


def _row_copy(src_hbm, row, dst, dst_row, sem):
    return pltpu.make_async_copy(src_hbm.at[pl.ds(row, 1), :], dst.at[pl.ds(dst_row, 1), :], sem)


def _dispatch_kernel(src_ref, x_hbm, g_ref, o_ref, buf, sem, *, rows):
    i = pl.program_id(0)
    n = pl.num_programs(0)

    def issue(tile, slot):
        def body(r, _):
            _row_copy(x_hbm, src_ref[tile * rows + r], buf.at[slot], r, sem.at[slot]).start()
            return 0
        lax.fori_loop(0, rows, body, 0)

    def wait(slot):
        def body(r, _):
            _row_copy(x_hbm, 0, buf.at[slot], r, sem.at[slot]).wait()
            return 0
        lax.fori_loop(0, rows, body, 0)

    slot = i % 2

    @pl.when(i == 0)
    def _():
        issue(0, 0)

    @pl.when(i + 1 < n)
    def _():
        issue(i + 1, 1 - slot)

    wait(slot)
    o_ref[...] = _rms(buf[slot], g_ref[...]).astype(BF16)


def _dispatch(src_tok, x, g, n_tiles, rows):
    t, d = x.shape
    kern = functools.partial(_dispatch_kernel, rows=rows)
    return pl.pallas_call(
        kern,
        grid_spec=pltpu.PrefetchScalarGridSpec(
            num_scalar_prefetch=1,
            grid=(n_tiles,),
            in_specs=[
                pl.BlockSpec(memory_space=pl.ANY),
                pl.BlockSpec((1, d), lambda i, s: (0, 0)),
            ],
            out_specs=pl.BlockSpec((rows, d), lambda i, s: (i, 0)),
            scratch_shapes=[pltpu.VMEM((2, rows, d), F32), pltpu.SemaphoreType.DMA((2,))],
        ),
        out_shape=jax.ShapeDtypeStruct((n_tiles * rows, d), BF16),
        compiler_params=_params("arbitrary"),
    )(src_tok, x, g.reshape(1, d))


def _gmm_gu_kernel(te_ref, nt_ref, x_ref, wg_ref, wu_ref, bg_ref, bu_ref, o_ref):
    i = pl.program_id(1)

    @pl.when(i < nt_ref[0])
    def _():
        x = x_ref[...]
        gate = jnp.dot(x, wg_ref[0], preferred_element_type=F32) + bg_ref[0]
        up = jnp.dot(x, wu_ref[0], preferred_element_type=F32) + bu_ref[0]
        gate = jnp.minimum(gate, SWIGLU_LIMIT)
        up = jnp.clip(up, -SWIGLU_LIMIT, SWIGLU_LIMIT)
        act = (up + 1.0) * (gate * jax.nn.sigmoid(SWIGLU_ALPHA * gate))
        o_ref[...] = act.astype(BF16)

    @pl.when(i >= nt_ref[0])
    def _():
        o_ref[...] = jnp.zeros(o_ref.shape, BF16)


def _gmm_gu(tile_exp, n_valid, xs, w_gu, b_gu3, tm, tn):
    p, d = xs.shape
    n_exp, _, f2 = w_gu.shape
    f = f2 // 2
    nj = f // tn
    return pl.pallas_call(
        _gmm_gu_kernel,
        grid_spec=pltpu.PrefetchScalarGridSpec(
            num_scalar_prefetch=2,
            grid=(nj, p // tm),
            in_specs=[
                pl.BlockSpec((tm, d), lambda j, i, te, nt: (i, 0)),
                pl.BlockSpec((1, d, tn), lambda j, i, te, nt: (te[i], 0, j)),
                pl.BlockSpec((1, d, tn), lambda j, i, te, nt: (te[i], 0, nj + j)),
                pl.BlockSpec((1, 1, tn), lambda j, i, te, nt: (te[i], 0, j)),
                pl.BlockSpec((1, 1, tn), lambda j, i, te, nt: (te[i], 0, nj + j)),
            ],
            out_specs=pl.BlockSpec((tm, tn), lambda j, i, te, nt: (i, j)),
        ),
        out_shape=jax.ShapeDtypeStruct((p, f), BF16),
        compiler_params=_params("arbitrary", "arbitrary"),
    )(tile_exp, n_valid, xs, w_gu, w_gu, b_gu3, b_gu3)


def _gmm_dn_kernel(te_ref, nt_ref, a_ref, w_ref, b_ref, o_ref):
    i = pl.program_id(1)

    @pl.when(i < nt_ref[0])
    def _():
        o_ref[...] = jnp.dot(a_ref[...], w_ref[0], preferred_element_type=F32) + b_ref[0]

    @pl.when(i >= nt_ref[0])
    def _():
        o_ref[...] = jnp.zeros(o_ref.shape, F32)


def _gmm_dn(tile_exp, n_valid, act, w_dn, b_dn3, tm, tn):
    p, f = act.shape
    d = w_dn.shape[2]
    return pl.pallas_call(
        _gmm_dn_kernel,
        grid_spec=pltpu.PrefetchScalarGridSpec(
            num_scalar_prefetch=2,
            grid=(d // tn, p // tm),
            in_specs=[
                pl.BlockSpec((tm, f), lambda j, i, te, nt: (i, 0)),
                pl.BlockSpec((1, f, tn), lambda j, i, te, nt: (te[i], 0, j)),
                pl.BlockSpec((1, 1, tn), lambda j, i, te, nt: (te[i], 0, j)),
            ],
            out_specs=pl.BlockSpec((tm, tn), lambda j, i, te, nt: (i, j)),
        ),
        out_shape=jax.ShapeDtypeStruct((p, d), F32),
        compiler_params=_params("arbitrary", "arbitrary"),
    )(tile_exp, n_valid, act, w_dn, b_dn3)


def _combine_kernel(pos_ref, x_ref, w_ref, y_hbm, o_ref, buf, sem, *, rows):
    i = pl.program_id(0)
    n = pl.num_programs(0)

    def issue(tile, slot):
        def body(r, _):
            base = (tile * rows + r) * TOP_K
            for k in range(TOP_K):
                _row_copy(y_hbm, pos_ref[base + k], buf.at[slot, k], r, sem.at[slot]).start()
            return 0
        lax.fori_loop(0, rows, body, 0)

    def wait(slot):
        def body(r, _):
            for k in range(TOP_K):
                _row_copy(y_hbm, 0, buf.at[slot, k], r, sem.at[slot]).wait()
            return 0
        lax.fori_loop(0, rows, body, 0)

    slot = i % 2

    @pl.when(i == 0)
    def _():
        issue(0, 0)

    @pl.when(i + 1 < n)
    def _():
        issue(i + 1, 1 - slot)

    wait(slot)
    w = w_ref[...]
    acc = x_ref[...]
    for k in range(TOP_K):
        acc = acc + w[:, k:k + 1] * buf[slot, k]
    o_ref[...] = acc


def _combine(pos, x, wgt, y, rows):
    t, d = x.shape
    kern = functools.partial(_combine_kernel, rows=rows)
    return pl.pallas_call(
        kern,
        grid_spec=pltpu.PrefetchScalarGridSpec(
            num_scalar_prefetch=1,
            grid=(t // rows,),
            in_specs=[
                pl.BlockSpec((rows, d), lambda i, s: (i, 0)),
                pl.BlockSpec((rows, wgt.shape[1]), lambda i, s: (i, 0)),
                pl.BlockSpec(memory_space=pl.ANY),
            ],
            out_specs=pl.BlockSpec((rows, d), lambda i, s: (i, 0)),
            scratch_shapes=[pltpu.VMEM((2, TOP_K, rows, d), F32),
                            pltpu.SemaphoreType.DMA((2,))],
        ),
        out_shape=jax.ShapeDtypeStruct((t, d), F32),
        compiler_params=_params("arbitrary"),
    )(pos, x, wgt, y)


def _route_plan(top_i, n_exp, tm, n_tiles):
    t = top_i.shape[0]
    a = t * TOP_K
    flat_e = top_i.reshape(a)
    onehot = (flat_e[:, None] == jnp.arange(n_exp, dtype=jnp.int32)[None, :]).astype(jnp.int32)
    csum = jnp.cumsum(onehot, axis=0)
    counts = csum[-1]
    rank = jnp.take_along_axis(csum, flat_e[:, None], axis=1)[:, 0] - 1
    padded = ((counts + tm - 1) // tm) * tm
    pend = jnp.cumsum(padded)
    pstart = pend - padded
    cstart = jnp.cumsum(counts) - counts
    n_valid = (pend[-1] // tm).astype(jnp.int32)
    pos = (pstart[flat_e] + rank).astype(jnp.int32)

    tile_ids = jnp.arange(n_tiles, dtype=jnp.int32)
    te = jnp.searchsorted(pend, tile_ids * tm, side="right").astype(jnp.int32)
    te = jnp.minimum(te, n_exp - 1)
    te = jnp.where(tile_ids < n_valid, te, te[jnp.maximum(n_valid - 1, 0)])

    order = jnp.argsort(flat_e, stable=True).astype(jnp.int32)
    p = jnp.arange(n_tiles * tm, dtype=jnp.int32)
    e_p = jnp.repeat(te, tm)
    r = p - pstart[e_p]
    valid = (r < counts[e_p]) & (p < pend[-1])
    src_sorted = jnp.clip(cstart[e_p] + r, 0, a - 1)
    src_tok = jnp.where(valid, order[src_sorted] // TOP_K, 0).astype(jnp.int32)
    return src_tok, pos, te, n_valid.reshape(1)


def _moe(x, g, w_router, b_router, w_gu, b_gu, w_dn, b_dn, cfg):
    t, d = x.shape
    n_exp = w_router.shape[1]
    tm = cfg["moe_tm"]
    n_tiles = (t * TOP_K + n_exp * (tm - 1) + tm - 1) // tm
    idx, wgt = _router(x, g, w_router, b_router, cfg["router_tm"])
    src_tok, pos, te, n_valid = _route_plan(idx[:, :TOP_K], n_exp, tm, n_tiles)
    xs = _dispatch(src_tok, x, g, n_tiles, tm)
    act = _gmm_gu(te, n_valid, xs, w_gu, b_gu.reshape(n_exp, 1, -1), tm, cfg["moe_tn"])
    y = _gmm_dn(te, n_valid, act, w_dn, b_dn.reshape(n_exp, 1, -1), tm, cfg["moe_tn"])
    return _combine(pos, x, wgt, y, cfg["combine_rows"])


def _rope_tables(pos):
    freqs = ROPE_THETA ** (-jnp.arange(0, ROT_DIM, 2, dtype=F32) / ROT_DIM)
    ang = pos.astype(F32)[:, None] * freqs[None, :]
    cos, sin = jnp.cos(ang), jnp.sin(ang)
    t = pos.shape[0]
    rest = HEAD_DIM - ROT_DIM
    c = jnp.concatenate([cos, cos, jnp.ones((t, rest), F32)], axis=1)
    s1 = jnp.concatenate([-sin, jnp.zeros((t, HEAD_DIM - ROT_HALF), F32)], axis=1)
    s2 = jnp.concatenate([jnp.zeros((t, ROT_HALF), F32), sin, jnp.zeros((t, rest), F32)], axis=1)
    return c, s1, s2


def _pick(n, prefs):
    for p in prefs:
        if n % p == 0:
            return p
    return n


def _forward(x_prompt, x_sample, cache_k, cache_v, norm1_g, norm2_g, w_in, q_norm_g, k_norm_g,
             lambda_qk, subln_g, gmlp_ln_g, gmlp_ln_b, gmlp_w_s, gmlp_b_s, gate_b, w_proj_a,
             w_proj_b, w_out, w_router, b_router, w_gu, b_gu, w_dn, b_dn):
    batch, seq, d = x_prompt.shape
    dec_batch, dec_seq, _ = x_sample.shape
    depth, _, past = cache_k.shape[:3]
    gw = gmlp_ln_g.shape[1]
    qk = (w_in.shape[2] - 2 * gw - 2 * d) // 3
    n_heads = qk // V_DIM
    tp, ts = batch * seq, dec_batch * dec_seq
    t_all = tp + ts
    assert tp % ts == 0 and seq % GMLP_CHUNK == 0 and past % GMLP_CHUNK == 0

    cfg = dict(
        tm=_pick(t_all, (512, 256, 128)),
        tn=_pick(qk, (512, 256)),
        tn_d=_pick(d, (512, 256)),
        tq=_pick(seq, (512, 256, 128)),
        router_tm=_pick(t_all, (256, 128)),
        moe_tm=256 if t_all >= 4096 else 64,
        moe_tn=_pick(d, (1024, 512, 256)),
        combine_rows=64,
        gmlp_sub=_pick(seq // GMLP_CHUNK, (4, 2, 1)),
    )
    tm, tn = cfg["tm"], cfg["tn"]

    pos_all = jnp.concatenate([jnp.tile(jnp.arange(seq), batch),
                               jnp.tile(past + jnp.arange(dec_seq), dec_batch)])
    rope = _rope_tables(pos_all)
    x = jnp.concatenate([x_prompt.reshape(tp, d), x_sample.reshape(ts, d)], axis=0)
    cache_k2 = cache_k.reshape(depth * dec_batch * past, qk)
    cache_v2 = cache_v.reshape(depth * dec_batch * past, qk)

    kp, vp, ks, vs, gs = [], [], [], [], []
    for l in range(depth):
        lam_init = 0.8 - 0.6 * math.exp(-0.3 * l)
        g_eff = (subln_g[l] * (1.0 - lam_init)).reshape(1, V_DIM)
        q_bf, k_f32, k_bf, v_f32, v_bf, u_act, vg_n, gates = _project_in(
            x, norm1_g[l], w_in[l].astype(BF16), q_norm_g[l], k_norm_g[l], rope,
            gmlp_ln_g[l], gmlp_ln_b[l], gate_b[l], (qk, gw), tm, tn)

        o_p = _attend_prompt(q_bf, k_bf, v_bf, lambda_qk[l], g_eff, lam_init, batch, seq,
                             cfg["tq"])
        o_s = _attend_sample(q_bf, k_bf, v_bf, cache_k2, cache_v2, l, lambda_qk[l], g_eff,
                             lam_init, tp, dec_batch, dec_seq, past)
        att = jnp.concatenate([o_p, o_s], axis=0)

        gm_p = _gmlp_mix(u_act, vg_n, gmlp_w_s[l], gmlp_b_s[l].T, 0, tp, GMLP_CHUNK,
                         cfg["gmlp_sub"])
        gm_s = _gmlp_mix(u_act, vg_n, gmlp_w_s[l][:, :dec_seq, :dec_seq],
                         gmlp_b_s[l][:, :dec_seq].T, tp, ts, dec_seq, dec_batch)
        gm = jnp.concatenate([gm_p, gm_s], axis=0)

        merged = _merge(att, gm, w_proj_a[l].astype(BF16), w_proj_b[l].astype(BF16), gates,
                        tm, cfg["tn_d"])
        x = _resid_matmul(x, merged, w_out[l].astype(BF16), tm, cfg["tn_d"])

        x = _moe(x, norm2_g[l], w_router[l], b_router[l], w_gu[l].astype(BF16), b_gu[l],
                 w_dn[l].astype(BF16), b_dn[l], cfg)

        kp.append(k_f32[:tp].reshape(batch, seq, n_heads, 2, HEAD_DIM))
        vp.append(v_f32[:tp].reshape(batch, seq, n_heads, V_DIM))
        ks.append(k_f32[tp:].reshape(dec_batch, dec_seq, n_heads, 2, HEAD_DIM))
        vs.append(v_f32[tp:].reshape(dec_batch, dec_seq, n_heads, V_DIM))
        gs.append(vg_n[tp:].reshape(dec_batch, dec_seq, gw))

    return (x[:tp].reshape(batch, seq, d), x[tp:].reshape(dec_batch, dec_seq, d),
            jnp.stack(kp), jnp.stack(vp), jnp.stack(ks), jnp.stack(vs), jnp.stack(gs))


def kernel(x_prompt, x_sample, cache_k, cache_v, norm1_g, norm2_g, w_in, q_norm_g, k_norm_g, lambda_qk, subln_g, gmlp_ln_g, gmlp_ln_b, gmlp_w_s, gmlp_b_s, gate_b, w_proj_a, w_proj_b, w_out, w_router, b_router, w_gu, b_gu, w_dn, b_dn):
    return _forward(x_prompt, x_sample, cache_k, cache_v, norm1_g, norm2_g, w_in, q_norm_g,
                    k_norm_g, lambda_qk, subln_g, gmlp_ln_g, gmlp_ln_b, gmlp_w_s, gmlp_b_s,
                    gate_b, w_proj_a, w_proj_b, w_out, w_router, b_router, w_gu, b_gu, w_dn,
                    b_dn)
```

```python
import functools
import math

import jax
import jax.numpy as jnp
from jax import lax
from jax.experimental import pallas as pl
from jax.experimental.pallas import tpu as pltpu

F32 = jnp.float32
BF16 = jnp.bfloat16
U32 = jnp.uint32

LANES = 128
HEAD_DIM = 128
V_DIM = 2 * HEAD_DIM
CHUNK = 64
GMLP_CHUNK = 128
ROT_DIM = HEAD_DIM // 4
ROT_HALF = ROT_DIM // 2
ROPE_THETA = 500000.0
TOP_K = 4
SWIGLU_LIMIT = 7.0
SWIGLU_ALPHA = 1.702
EPS = 1e-6
NEG = -1e30
ATT_SCALE = 1.0 / math.sqrt(HEAD_DIM)
LOG2E = math.log2(math.e)
Q_SCALE = ATT_SCALE * LOG2E
GELU_C = math.sqrt(2.0 / math.pi)
SCORE_BOUND_LIMIT = 64.0
HI_MASK = 0xFFFF0000

VMEM_LIMIT = 52 * 1024 * 1024


def _params(*sem):
    return pltpu.CompilerParams(dimension_semantics=sem, vmem_limit_bytes=VMEM_LIMIT)


def _gelu(x):
    return 0.5 * x * (1.0 + jnp.tanh(GELU_C * (x + 0.044715 * (x * x * x))))


def _rms(x, g):
    return x * lax.rsqrt(jnp.mean(x * x, axis=-1, keepdims=True) + EPS) * g


def _pick(n, prefs):
    for p in prefs:
        if n % p == 0:
            return p
    return n


def _prenorm_kernel(x_ref, g_ref, o_ref):
    o_ref[...] = _rms(x_ref[...], g_ref[...]).astype(BF16)


def _prenorm(x, g, tm):
    t, d = x.shape
    return pl.pallas_call(
        _prenorm_kernel,
        grid=(t // tm,),
        in_specs=[pl.BlockSpec((tm, d), lambda i: (i, 0)),
                  pl.BlockSpec((1, d), lambda i: (0, 0))],
        out_specs=pl.BlockSpec((tm, d), lambda i: (i, 0)),
        out_shape=jax.ShapeDtypeStruct((t, d), BF16),
        compiler_params=_params("parallel"),
        name="prenorm",
    )(x, g.reshape(1, d))


def _mm_kernel(*refs, epilogue, n_extra, n_alias):
    h_ref, w_ref = refs[:2]
    extra = refs[2:2 + n_extra]
    outs = refs[2 + n_extra + n_alias:]
    z = jnp.dot(h_ref[...], w_ref[...], preferred_element_type=F32)
    epilogue(z, pl.program_id(1), extra, outs)


def _matmul_epi(name, h, w, col0, ncols, tm, tn, epilogue, extras, extra_specs,
                out_shapes, out_specs, row0=0, rows=None, aliased=()):
    t, d = h.shape
    rows = t - row0 if rows is None else rows
    assert rows % tm == 0 and row0 % tm == 0 and ncols % tn == 0 and col0 % tn == 0
    rb0, jb = row0 // tm, col0 // tn
    in_specs = [
        pl.BlockSpec((tm, d), lambda i, j: (rb0 + i, 0)),
        pl.BlockSpec((d, tn), lambda i, j: (0, jb + j)),
    ] + list(extra_specs) + [pl.BlockSpec(memory_space=pl.ANY)] * len(aliased)
    n_in = 2 + len(extras)
    kern = functools.partial(_mm_kernel, epilogue=epilogue, n_extra=len(extras),
                             n_alias=len(aliased))
    return pl.pallas_call(
        kern,
        grid=(rows // tm, ncols // tn),
        in_specs=in_specs,
        out_specs=out_specs,
        out_shape=out_shapes,
        input_output_aliases={n_in + a: a for a in range(len(aliased))},
        compiler_params=_params("parallel", "arbitrary"),
        name=name,
    )(h, w, *extras, *aliased)


def _headnorm_rope(zc, mix, ct, st):
    lhs = jnp.concatenate([(zc * zc).astype(BF16), zc.astype(BF16)], axis=1)
    red = jnp.dot(lhs, mix, preferred_element_type=F32)
    rstd = lax.rsqrt(red[:, :HEAD_DIM] * (1.0 / HEAD_DIM) + EPS)
    return (zc * ct + red[:, HEAD_DIM:] * st) * rstd


def _head_slices(tn):
    return [slice(h * HEAD_DIM, (h + 1) * HEAD_DIM) for h in range(tn // HEAD_DIM)]


def _store_token_major(out_ref, j, n_tiles, pieces, tiles, dest):
    tm = pieces[0].shape[0]
    for jj in range(n_tiles):
        @pl.when(j == jj)
        def _(jj=jj):
            for hh, r in enumerate(pieces):
                out_ref[pl.ds(dest(jj * len(pieces) + hh), tm, stride=tiles), :] = r


def _epi_q(z, j, extra, outs, *, tn):
    mix, ct, st = (r[...] for r in extra)
    for sl in _head_slices(tn):
        outs[0][:, sl] = _headnorm_rope(z[:, sl], mix, ct, st).astype(BF16)


def _epi_k_prompt(z, j, extra, outs, *, tn, n_tiles):
    mix, ct, st = (r[...] for r in extra)
    kout_ref, kt_ref = outs
    pieces = []
    for sl in _head_slices(tn):
        r = _headnorm_rope(z[:, sl], mix, ct, st)
        kt_ref[sl, :] = r.T.astype(BF16)
        pieces.append(r)
    _store_token_major(kout_ref, j, n_tiles, pieces, n_tiles * len(pieces), lambda c: c)


def _epi_k_sample(z, j, extra, outs, *, tn):
    mix, ct, st = (r[...] for r in extra)
    for sl in _head_slices(tn):
        outs[0][:, sl] = _headnorm_rope(z[:, sl], mix, ct, st)


def _epi_v_prompt(z, j, extra, outs, *, tn, n_tiles):
    vout_ref, vbf_ref = outs
    vbf_ref[...] = z.astype(BF16)
    pieces = [z[:, sl] for sl in _head_slices(tn)]
    tiles = n_tiles * len(pieces)
    per_head = V_DIM // LANES
    _store_token_major(vout_ref, j, n_tiles, pieces, tiles,
                       lambda c: (c % per_head) * (tiles // per_head) + c // per_head)


def _epi_copy(z, j, extra, outs):
    for o in outs:
        o[...] = z.astype(o.dtype)


def _epi_gelu(z, j, extra, outs):
    outs[0][...] = _gelu(z).astype(outs[0].dtype)


def _epi_gelu_layernorm(z, j, extra, outs, *, tn, n_tiles):
    o = outs[0]
    a = _gelu(z)
    for jj in range(n_tiles):
        @pl.when(j == jj)
        def _(jj=jj):
            o[:, jj * tn:(jj + 1) * tn] = a

    @pl.when(j == n_tiles - 1)
    def _():
        full = o[...]
        mu = jnp.mean(full, axis=-1, keepdims=True)
        cen = full - mu
        var = jnp.mean(cen * cen, axis=-1, keepdims=True)
        o[...] = cen * lax.rsqrt(var + EPS) * extra[0][...] + extra[1][...]


def _epi_sigmoid_bias(z, j, extra, outs):
    outs[0][...] = jax.nn.sigmoid(z + extra[0][...]).astype(outs[0].dtype)


def _project_in(h, w_in, qg, kg, rope, ln_g, ln_b, gate_b, dims, cfg, layer, tp, kout, vout):
    t, d = h.shape
    ts = t - tp
    qk, gw = dims
    tm, tn, tm_p, tm_s = cfg["tm"], cfg["tn"], cfg["tm_p"], cfg["tm_s"]
    cos_tab, sin_tab = rope
    o1, o2, o3, o4, o5 = qk, 2 * qk, 3 * qk, 3 * qk + gw, 3 * qk + 2 * gw
    tiles = qk // LANES
    n_tiles = qk // tn
    vec = lambda n: pl.BlockSpec((1, n), lambda i, j: (0, 0))
    out_tile = lambda tmx: pl.BlockSpec((tmx, tn), lambda i, j: (i, j))
    sds = jax.ShapeDtypeStruct
    mix = _mix_matrix()

    def rope_specs(tmx, row0):
        rb0 = row0 // tmx
        row_tile = pl.BlockSpec((tmx, HEAD_DIM), lambda i, j: (rb0 + i, 0))
        return [pl.BlockSpec((2 * HEAD_DIM, 2 * HEAD_DIM), lambda i, j: (0, 0)),
                row_tile, row_tile]

    def gain_tables(gain, scale):
        lane = jnp.arange(HEAD_DIM)
        partner = jnp.where(lane < ROT_HALF, jnp.roll(gain, -ROT_HALF), jnp.roll(gain, ROT_HALF))
        return cos_tab * (gain * scale)[None, :], sin_tab * (partner * scale)[None, :]

    stacked = lambda: pl.BlockSpec((tm_p * tiles, LANES),
                                   lambda i, j: (layer * (tp // tm_p) + i, 0))
    k_tabs = gain_tables(kg, 1.0)

    (q_bf,) = _matmul_epi(
        "proj_q", h, w_in, 0, qk, tm, tn, functools.partial(_epi_q, tn=tn),
        [mix, *gain_tables(qg, Q_SCALE)], rope_specs(tm, 0),
        [sds((t, qk), BF16)], [out_tile(tm)])
    kout, kt_bf = _matmul_epi(
        "proj_k_prompt", h, w_in, o1, qk, tm_p, tn,
        functools.partial(_epi_k_prompt, tn=tn, n_tiles=n_tiles),
        [mix, *k_tabs], rope_specs(tm_p, 0),
        [sds(kout.shape, F32), sds((qk, tp), BF16)],
        [stacked(), pl.BlockSpec((tn, tm_p), lambda i, j: (j, i))],
        rows=tp, aliased=(kout,))
    (k_s,) = _matmul_epi(
        "proj_k_sample", h, w_in, o1, qk, tm_s, tn, functools.partial(_epi_k_sample, tn=tn),
        [mix, *k_tabs], rope_specs(tm_s, tp),
        [sds((ts, qk), F32)], [out_tile(tm_s)], row0=tp)
    vout, v_bf = _matmul_epi(
        "proj_v_prompt", h, w_in, o2, qk, tm_p, tn,
        functools.partial(_epi_v_prompt, tn=tn, n_tiles=n_tiles), [], [],
        [sds(vout.shape, F32), sds((tp, qk), BF16)], [stacked(), out_tile(tm_p)],
        rows=tp, aliased=(vout,))
    (v_s,) = _matmul_epi(
        "proj_v_sample", h, w_in, o2, qk, tm_s, tn, _epi_copy, [], [],
        [sds((ts, qk), F32)], [out_tile(tm_s)], row0=tp)
    (u_act,) = _matmul_epi(
        "proj_u", h, w_in, o3, gw, tm, tn, _epi_gelu, [], [],
        [sds((t, gw), BF16)], [out_tile(tm)])
    (vg_n,) = _matmul_epi(
        "proj_vg", h, w_in, o4, gw, tm, tn,
        functools.partial(_epi_gelu_layernorm, tn=tn, n_tiles=gw // tn),
        [ln_g.reshape(1, gw), ln_b.reshape(1, gw)], [vec(gw), vec(gw)],
        [sds((t, gw), F32)], [pl.BlockSpec((tm, gw), lambda i, j: (i, 0))])
    (gates,) = _matmul_epi(
        "proj_gates", h, w_in, o5, 2 * d, tm, tn, _epi_sigmoid_bias,
        [gate_b.reshape(1, 2 * d)], [pl.BlockSpec((1, tn), lambda i, j: (0, j))],
        [sds((t, 2 * d), BF16)], [out_tile(tm)])
    return q_bf, kt_bf, v_bf, k_s, v_s, u_act, vg_n, gates, kout, vout


def _lambda(lq_ref, lam_init):
    lq = lq_ref[...]
    a = jnp.sum(lq[0:1] * lq[1:2], axis=-1, keepdims=True)
    b = jnp.sum(lq[2:3] * lq[3:4], axis=-1, keepdims=True)
    return jnp.exp(a) - jnp.exp(b) + lam_init


def _attn_finish(lam, g, a1, l1, a2, l2):
    o = a1 / l1 - lam * (a2 / l2)
    return _rms(o, g).astype(BF16)


def _attn_prompt_kernel(lq_ref, g_ref, q_ref, kt_ref, v_ref, o_ref, *scratch,
                        tq, lam_init, pair, running_max):
    qi = pl.program_id(2)
    half = tq // 2
    state = (scratch[0::2], scratch[1::2])

    def block(off, width, r0, diag_col0):
        rows = tq - r0
        v = v_ref[pl.ds(off, width), :]
        for mp in range(2):
            a_ref, l_ref = state[mp][:2]
            q = q_ref[r0:, mp * HEAD_DIM:(mp + 1) * HEAD_DIM]
            kt = kt_ref[mp * HEAD_DIM:(mp + 1) * HEAD_DIM, pl.ds(off, width)]
            s = jnp.dot(q, kt, preferred_element_type=F32)
            vis = None
            if diag_col0 is not None:
                rc = (lax.broadcasted_iota(jnp.int32, (rows, width), 0) + r0) // CHUNK
                cc = (lax.broadcasted_iota(jnp.int32, (rows, width), 1) + diag_col0) // CHUNK
                vis = cc <= rc
            if running_max:
                m_ref = state[mp][2]
                if vis is not None:
                    s = jnp.where(vis, s, NEG)
                m_old = m_ref[r0:, :]
                m_new = jnp.maximum(m_old, jnp.max(s, axis=-1, keepdims=True))
                alpha = jnp.exp2(m_old - m_new)
                p = jnp.exp2(s - m_new)
                m_ref[r0:, :] = m_new
                l_ref[r0:, :] = alpha * l_ref[r0:, :] + jnp.sum(p, axis=-1, keepdims=True)
                a_ref[r0:, :] = alpha * a_ref[r0:, :] + jnp.dot(
                    p.astype(BF16), v, preferred_element_type=F32)
            else:
                p = jnp.exp2(s)
                if vis is not None:
                    p = jnp.where(vis, p, 0.0)
                l_ref[r0:, :] += jnp.sum(p, axis=-1, keepdims=True)
                a_ref[r0:, :] += jnp.dot(p.astype(BF16), v, preferred_element_type=F32)

    for refs in state:
        refs[0][...] = jnp.zeros(refs[0].shape, F32)
        refs[1][...] = jnp.zeros(refs[1].shape, F32)
        if running_max:
            refs[2][...] = jnp.full(refs[2].shape, NEG, F32)

    def body(j, _):
        off = pl.multiple_of(j * (pair * tq), pair * tq)
        for u in range(pair):
            block(off + u * tq, tq, 0, None)
        return 0

    lax.fori_loop(0, qi // pair, body, 0)
    for u in range(pair - 1):
        @pl.when(qi % pair > u)
        def _(u=u):
            block(pl.multiple_of((qi - 1 - u) * tq, tq), tq, 0, None)

    base = pl.multiple_of(qi * tq, tq)
    block(base, half, 0, 0)
    block(base + half, half, half, half)
    (a1, l1), (a2, l2) = (r[:2] for r in state)
    o_ref[...] = _attn_finish(_lambda(lq_ref, lam_init), g_ref[...], a1[...], l1[...],
                              a2[...], l2[...])


def _attend_prompt(q_bf, kt_bf, v_bf, lq, g_eff, lam_init, batch, seq, tq, pair, running_max):
    qk = q_bf.shape[1]
    n_heads = qk // V_DIM
    nq = seq // tq
    kern = functools.partial(_attn_prompt_kernel, tq=tq, lam_init=lam_init, pair=pair,
                             running_max=running_max)
    per_map = [pltpu.VMEM((tq, V_DIM), F32), pltpu.VMEM((tq, 1), F32)]
    if running_max:
        per_map.append(pltpu.VMEM((tq, 1), F32))
    scratch = [s for s in per_map for _ in range(2)]
    return pl.pallas_call(
        kern,
        grid=(batch, n_heads, nq),
        in_specs=[
            pl.BlockSpec((4, HEAD_DIM), lambda b, h, i: (0, 0)),
            pl.BlockSpec((1, V_DIM), lambda b, h, i: (0, 0)),
            pl.BlockSpec((tq, V_DIM), lambda b, h, i: (b * nq + i, h)),
            pl.BlockSpec((V_DIM, seq), lambda b, h, i: (h, b)),
            pl.BlockSpec((seq, V_DIM), lambda b, h, i: (b, h)),
        ],
        out_specs=pl.BlockSpec((tq, V_DIM), lambda b, h, i: (b * nq + i, h)),
        out_shape=jax.ShapeDtypeStruct((batch * seq, qk), BF16),
        scratch_shapes=scratch,
        compiler_params=_params("parallel", "parallel", "arbitrary"),
        name="attn_prompt_max" if running_max else "attn_prompt",
    )(lq, g_eff, q_bf, kt_bf, v_bf)


def _attn_sample_kernel(lq_ref, g_ref, q_ref, kc_ref, vc_ref, kn_ref, vn_ref, o_ref, *,
                        lam_init):
    q = q_ref[...]
    kc = kc_ref[...].astype(BF16)
    vc = vc_ref[...].astype(BF16)
    kn = kn_ref[...].astype(BF16)
    vn = vn_ref[...].astype(BF16)
    res = []
    for mp in range(2):
        sl = slice(mp * HEAD_DIM, (mp + 1) * HEAD_DIM)
        sc = lax.dot_general(q[:, sl], kc[:, sl], (((1,), (1,)), ((), ())),
                             preferred_element_type=F32)
        sn = lax.dot_general(q[:, sl], kn[:, sl], (((1,), (1,)), ((), ())),
                             preferred_element_type=F32)
        m = jnp.maximum(jnp.max(sc, axis=-1, keepdims=True),
                        jnp.max(sn, axis=-1, keepdims=True))
        pc = jnp.exp2(sc - m)
        pn = jnp.exp2(sn - m)
        l = jnp.sum(pc, axis=-1, keepdims=True) + jnp.sum(pn, axis=-1, keepdims=True)
        acc = (jnp.dot(pc.astype(BF16), vc, preferred_element_type=F32)
               + jnp.dot(pn.astype(BF16), vn, preferred_element_type=F32))
        res += [acc, l]
    o_ref[...] = _attn_finish(_lambda(lq_ref, lam_init), g_ref[...], *res)


def _attend_sample(q_bf, k_s, v_s, cache_k2, cache_v2, layer, lq, g_eff, lam_init,
                   row0, dec_batch, dec_seq, past):
    qk = q_bf.shape[1]
    n_heads = qk // V_DIM
    rb0 = row0 // dec_seq
    q_rows = pl.BlockSpec((dec_seq, V_DIM), lambda b, h: (rb0 + b, h))
    new_rows = pl.BlockSpec((dec_seq, V_DIM), lambda b, h: (b, h))
    cache_rows = pl.BlockSpec((past, V_DIM), lambda b, h: (layer * dec_batch + b, h))
    kern = functools.partial(_attn_sample_kernel, lam_init=lam_init)
    return pl.pallas_call(
        kern,
        grid=(dec_batch, n_heads),
        in_specs=[
            pl.BlockSpec((4, HEAD_DIM), lambda b, h: (0, 0)),
            pl.BlockSpec((1, V_DIM), lambda b, h: (0, 0)),
            q_rows, cache_rows, cache_rows, new_rows, new_rows,
        ],
        out_specs=pl.BlockSpec((dec_seq, V_DIM), lambda b, h: (b, h)),
        out_shape=jax.ShapeDtypeStruct((dec_batch * dec_seq, qk), BF16),
        compiler_params=_params("parallel", "parallel"),
        name="attn_sample",
    )(lq, g_eff, q_bf, cache_k2, cache_v2, k_s, v_s)


def _gmlp_kernel(w_ref, b_ref, u_ref, v_ref, o_ref, *, c, n_sub, groups):
    gd = u_ref.shape[1] // groups
    tri = (lax.broadcasted_iota(jnp.int32, (c, c), 0)
           >= lax.broadcasted_iota(jnp.int32, (c, c), 1))
    for g in range(groups):
        wg = jnp.where(tri, w_ref[g], 0.0).astype(BF16)
        bg = b_ref[:, g:g + 1]
        cs = slice(g * gd, (g + 1) * gd)
        for s in range(n_sub):
            rs = slice(s * c, (s + 1) * c)
            mixed = jnp.dot(wg, v_ref[rs, cs].astype(BF16), preferred_element_type=F32) + bg
            o_ref[rs, cs] = (u_ref[rs, cs].astype(F32) * mixed).astype(BF16)


def _gmlp_mix(name, u, vg, w_s, b_s_t, row0, rows, c, n_sub):
    groups = w_s.shape[0]
    gw = u.shape[1]
    tr = c * n_sub
    rb0 = row0 // tr
    rows_spec = pl.BlockSpec((tr, gw), lambda i: (rb0 + i, 0))
    kern = functools.partial(_gmlp_kernel, c=c, n_sub=n_sub, groups=groups)
    return pl.pallas_call(
        kern,
        grid=(rows // tr,),
        in_specs=[
            pl.BlockSpec((groups, c, c), lambda i: (0, 0, 0)),
            pl.BlockSpec((c, groups), lambda i: (0, 0)),
            rows_spec, rows_spec,
        ],
        out_specs=pl.BlockSpec((tr, gw), lambda i: (i, 0)),
        out_shape=jax.ShapeDtypeStruct((rows, gw), BF16),
        compiler_params=_params("parallel"),
        name=name,
    )(w_s, b_s_t, u, vg)


def _merge_kernel(a_ref, m_ref, wa_ref, wb_ref, g0_ref, g1_ref, o_ref):
    ya = jnp.dot(a_ref[...], wa_ref[...], preferred_element_type=F32)
    yb = jnp.dot(m_ref[...], wb_ref[...], preferred_element_type=F32)
    o_ref[...] = (g0_ref[...].astype(F32) * ya + g1_ref[...].astype(F32) * yb).astype(BF16)


def _merge(att, gm, w_pa, w_pb, gates, tm, tn):
    t, wa = att.shape
    d = w_pa.shape[1]
    nj = d // tn
    return pl.pallas_call(
        _merge_kernel,
        grid=(t // tm, nj),
        in_specs=[
            pl.BlockSpec((tm, wa), lambda i, j: (i, 0)),
            pl.BlockSpec((tm, gm.shape[1]), lambda i, j: (i, 0)),
            pl.BlockSpec((wa, tn), lambda i, j: (0, j)),
            pl.BlockSpec((gm.shape[1], tn), lambda i, j: (0, j)),
            pl.BlockSpec((tm, tn), lambda i, j: (i, j)),
            pl.BlockSpec((tm, tn), lambda i, j: (i, nj + j)),
        ],
        out_specs=pl.BlockSpec((tm, tn), lambda i, j: (i, j)),
        out_shape=jax.ShapeDtypeStruct((t, d), BF16),
        compiler_params=_params("parallel", "arbitrary"),
        name="merge",
    )(att, gm, w_pa, w_pb, gates, gates)


def _resid_matmul_kernel(x_ref, m_ref, w_ref, o_ref):
    o_ref[...] = x_ref[...] + jnp.dot(m_ref[...], w_ref[...], preferred_element_type=F32)


def _resid_matmul(x, m, w, tm, tn):
    t, d = x.shape
    return pl.pallas_call(
        _resid_matmul_kernel,
        grid=(t // tm, d // tn),
        in_specs=[
            pl.BlockSpec((tm, tn), lambda i, j: (i, j)),
            pl.BlockSpec((tm, m.shape[1]), lambda i, j: (i, 0)),
            pl.BlockSpec((m.shape[1], tn), lambda i, j: (0, j)),
        ],
        out_specs=pl.BlockSpec((tm, tn), lambda i, j: (i, j)),
        out_shape=jax.ShapeDtypeStruct((t, d), F32),
        compiler_params=_params("parallel", "arbitrary"),
        name="out_proj",
    )(x, m, w)


def _router_kernel(x_ref, g_ref, wr_ref, br_ref, idx_ref, wgt_ref, hp_ref):
    tm, d = x_ref.shape
    h = _rms(x_ref[...], g_ref[...])
    logits = jnp.dot(h, wr_ref[...], preferred_element_type=F32,
                     precision=lax.Precision.HIGHEST) + br_ref[...]
    n_exp = logits.shape[1]
    lane_e = lax.broadcasted_iota(jnp.int32, logits.shape, 1)
    lane_o = lax.broadcasted_iota(jnp.int32, idx_ref.shape, 1)
    idx_out = jnp.zeros(idx_ref.shape, jnp.int32)
    val_out = jnp.full(wgt_ref.shape, -jnp.inf, F32)
    top = None
    for k in range(TOP_K):
        m = jnp.max(logits, axis=-1, keepdims=True)
        i = jnp.min(jnp.where(logits == m, lane_e, n_exp), axis=-1, keepdims=True)
        logits = jnp.where(lane_e == i, -jnp.inf, logits)
        idx_out = jnp.where(lane_o == k, i, idx_out)
        val_out = jnp.where(lane_o == k, m, val_out)
        if k == 0:
            top = m
    e = jnp.exp(val_out - top)
    idx_ref[...] = idx_out
    wgt_ref[...] = e / jnp.sum(e, axis=-1, keepdims=True)

    nw = d // (2 * LANES)
    bits = lax.bitcast_convert_type(h.astype(BF16).astype(F32), U32)
    for c in range(nw):
        lo = bits[:, c * LANES:(c + 1) * LANES] >> 16
        hi = bits[:, (c + nw) * LANES:(c + nw + 1) * LANES] & jnp.uint32(HI_MASK)
        hp_ref[pl.ds(c, tm, stride=nw), :] = lo | hi


def _router(x, g, w_router, b_router, tm):
    t, d = x.shape
    n_exp = w_router.shape[1]
    nw = d // (2 * LANES)
    return pl.pallas_call(
        _router_kernel,
        grid=(t // tm,),
        in_specs=[
            pl.BlockSpec((tm, d), lambda i: (i, 0)),
            pl.BlockSpec((1, d), lambda i: (0, 0)),
            pl.BlockSpec((d, n_exp), lambda i: (0, 0)),
            pl.BlockSpec((1, n_exp), lambda i: (0, 0)),
        ],
        out_specs=[pl.BlockSpec((tm, LANES), lambda i: (i, 0)),
                   pl.BlockSpec((tm, LANES), lambda i: (i, 0)),
                   pl.BlockSpec((tm * nw, LANES), lambda i: (i, 0))],
        out_shape=[jax.ShapeDtypeStruct((t, LANES), jnp.int32),
                   jax.ShapeDtypeStruct((t, LANES), F32),
                   jax.ShapeDtypeStruct((t * nw, LANES), U32)],
        compiler_params=_params("parallel"),
        name="router",
    )(x, g.reshape(1, d), w_router, b_router.reshape(1, n_exp))


ISSUE_UNROLL = 8


def _dispatch_kernel(src_ref, hp_hbm, o_ref, buf, sem, *, rows, nw):
    i = pl.program_id(0)
    n = pl.num_programs(0)

    def row_copy(tok_row, slot, r):
        return pltpu.make_async_copy(hp_hbm.at[pl.ds(tok_row, nw), :],
                                     buf.at[slot, pl.ds(r * nw, nw), :], sem.at[slot])

    def issue(tile, slot):
        def body(rb, _):
            for u in range(ISSUE_UNROLL):
                r = rb * ISSUE_UNROLL + u
                tok_row = pl.multiple_of(src_ref[tile * rows + r] * nw, nw)
                row_copy(tok_row, slot, r).start()
            return 0
        lax.fori_loop(0, rows // ISSUE_UNROLL, body, 0)

    slot = i % 2

    @pl.when(i == 0)
    def _():
        issue(0, 0)

    @pl.when(i + 1 < n)
    def _():
        issue(i + 1, 1 - slot)

    for r in range(rows):
        row_copy(0, slot, r).wait()

    los, his = [], []
    for c in range(nw):
        w = buf[slot, pl.ds(c, rows, stride=nw), :]
        los.append(lax.bitcast_convert_type(w << 16, F32))
        his.append(lax.bitcast_convert_type(w & jnp.uint32(HI_MASK), F32))
    o_ref[...] = jnp.concatenate(los + his, axis=1).astype(BF16)


def _dispatch(src_tok, hp, d, n_tiles, rows):
    nw = d // (2 * LANES)
    kern = functools.partial(_dispatch_kernel, rows=rows, nw=nw)
    return pl.pallas_call(
        kern,
        grid_spec=pltpu.PrefetchScalarGridSpec(
            num_scalar_prefetch=1,
            grid=(n_tiles,),
            in_specs=[pl.BlockSpec(memory_space=pl.ANY)],
            out_specs=pl.BlockSpec((rows, d), lambda i, s: (i, 0)),
            scratch_shapes=[pltpu.VMEM((2, rows * nw, LANES), U32),
                            pltpu.SemaphoreType.DMA((2,))],
        ),
        out_shape=jax.ShapeDtypeStruct((n_tiles * rows, d), BF16),
        compiler_params=_params("arbitrary"),
        name="dispatch",
    )(src_tok, hp)


def _gmm_gu_kernel(te_ref, nt_ref, x_ref, wg_ref, wu_ref, bg_ref, bu_ref, o_ref):
    i = pl.program_id(1)

    @pl.when(i < nt_ref[0])
    def _():
        x = x_ref[...]
        gate = jnp.dot(x, wg_ref[0], preferred_element_type=F32) + bg_ref[0]
        up = jnp.dot(x, wu_ref[0], preferred_element_type=F32) + bu_ref[0]
        gate = jnp.minimum(gate, SWIGLU_LIMIT)
        up = jnp.clip(up, -SWIGLU_LIMIT, SWIGLU_LIMIT)
        act = (up + 1.0) * (gate * jax.nn.sigmoid(SWIGLU_ALPHA * gate))
        o_ref[...] = act.astype(BF16)

    @pl.when(i >= nt_ref[0])
    def _():
        o_ref[...] = jnp.zeros(o_ref.shape, BF16)


def _gmm_gu(tile_exp, n_valid, xs, w_gu, b_gu3, tm, tn):
    p, d = xs.shape
    f = w_gu.shape[2] // 2
    nj = f // tn
    return pl.pallas_call(
        _gmm_gu_kernel,
        grid_spec=pltpu.PrefetchScalarGridSpec(
            num_scalar_prefetch=2,
            grid=(nj, p // tm),
            in_specs=[
                pl.BlockSpec((tm, d), lambda j, i, te, nt: (i, 0)),
                pl.BlockSpec((1, d, tn), lambda j, i, te, nt: (te[i], 0, j)),
                pl.BlockSpec((1, d, tn), lambda j, i, te, nt: (te[i], 0, nj + j)),
                pl.BlockSpec((1, 1, tn), lambda j, i, te, nt: (te[i], 0, j)),
                pl.BlockSpec((1, 1, tn), lambda j, i, te, nt: (te[i], 0, nj + j)),
            ],
            out_specs=pl.BlockSpec((tm, tn), lambda j, i, te, nt: (i, j)),
        ),
        out_shape=jax.ShapeDtypeStruct((p, f), BF16),
        compiler_params=_params("arbitrary", "arbitrary"),
        name="expert_gate_up",
    )(tile_exp, n_valid, xs, w_gu, w_gu, b_gu3, b_gu3)


def _gmm_dn_kernel(te_ref, nt_ref, a_ref, w_ref, b_ref, o_ref):
    i = pl.program_id(1)

    @pl.when(i < nt_ref[0])
    def _():
        o_ref[...] = jnp.dot(a_ref[...], w_ref[0], preferred_element_type=F32) + b_ref[0]

    @pl.when(i >= nt_ref[0])
    def _():
        o_ref[...] = jnp.zeros(o_ref.shape, F32)


def _gmm_dn(tile_exp, n_valid, act, w_dn, b_dn3, tm, tn):
    p, f = act.shape
    d = w_dn.shape[2]
    return pl.pallas_call(
        _gmm_dn_kernel,
        grid_spec=pltpu.PrefetchScalarGridSpec(
            num_scalar_prefetch=2,
            grid=(d // tn, p // tm),
            in_specs=[
                pl.BlockSpec((tm, f), lambda j, i, te, nt: (i, 0)),
                pl.BlockSpec((1, f, tn), lambda j, i, te, nt: (te[i], 0, j)),
                pl.BlockSpec((1, 1, tn), lambda j, i, te, nt: (te[i], 0, j)),
            ],
            out_specs=pl.BlockSpec((tm, tn), lambda j, i, te, nt: (i, j)),
        ),
        out_shape=jax.ShapeDtypeStruct((p, d), F32),
        compiler_params=_params("arbitrary", "arbitrary"),
        name="expert_down",
    )(tile_exp, n_valid, act, w_dn, b_dn3)


def _combine_kernel(pos_ref, x_ref, w_ref, g_ref, y_hbm, o_ref, h_ref, buf, sem, *, rows):
    i = pl.program_id(0)
    n = pl.num_programs(0)

    def row_copy(src_row, slot, k, r):
        return pltpu.make_async_copy(y_hbm.at[pl.ds(src_row, 1), :],
                                     buf.at[slot, k, pl.ds(r, 1), :], sem.at[slot])

    def issue(tile, slot):
        def body(rb, _):
            for u in range(ISSUE_UNROLL // TOP_K):
                r = rb * (ISSUE_UNROLL // TOP_K) + u
                base = (tile * rows + r) * TOP_K
                for k in range(TOP_K):
                    row_copy(pos_ref[base + k], slot, k, r).start()
            return 0
        lax.fori_loop(0, rows * TOP_K // ISSUE_UNROLL, body, 0)

    slot = i % 2

    @pl.when(i == 0)
    def _():
        issue(0, 0)

    @pl.when(i + 1 < n)
    def _():
        issue(i + 1, 1 - slot)

    for r in range(rows):
        for k in range(TOP_K):
            row_copy(0, slot, k, r).wait()

    w = w_ref[...]
    acc = x_ref[...]
    for k in range(TOP_K):
        acc = acc + w[:, k:k + 1] * buf[slot, k]
    o_ref[...] = acc
    h_ref[...] = _rms(acc, g_ref[...]).astype(BF16)


def _combine(pos, x, wgt, g_next, y, rows):
    t, d = x.shape
    kern = functools.partial(_combine_kernel, rows=rows)
    return pl.pallas_call(
        kern,
        grid_spec=pltpu.PrefetchScalarGridSpec(
            num_scalar_prefetch=1,
            grid=(t // rows,),
            in_specs=[
                pl.BlockSpec((rows, d), lambda i, s: (i, 0)),
                pl.BlockSpec((rows, wgt.shape[1]), lambda i, s: (i, 0)),
                pl.BlockSpec((1, d), lambda i, s: (0, 0)),
                pl.BlockSpec(memory_space=pl.ANY),
            ],
            out_specs=[pl.BlockSpec((rows, d), lambda i, s: (i, 0)),
                       pl.BlockSpec((rows, d), lambda i, s: (i, 0))],
            scratch_shapes=[pltpu.VMEM((2, TOP_K, rows, d), F32),
                            pltpu.SemaphoreType.DMA((2,))],
        ),
        out_shape=[jax.ShapeDtypeStruct((t, d), F32), jax.ShapeDtypeStruct((t, d), BF16)],
        compiler_params=_params("arbitrary"),
        name="combine",
    )(pos, x, wgt, g_next.reshape(1, d), y)


def _route_plan(top_i, n_exp, tm, n_tiles):
    t = top_i.shape[0]
    a = t * TOP_K
    flat_e = top_i.reshape(a)
    onehot = (flat_e[:, None] == jnp.arange(n_exp, dtype=jnp.int32)[None, :]).astype(jnp.int32)
    csum = jnp.cumsum(onehot, axis=0)
    counts = csum[-1]
    rank = jnp.take_along_axis(csum, flat_e[:, None], axis=1)[:, 0] - 1
    padded = ((counts + tm - 1) // tm) * tm
    pend = jnp.cumsum(padded)
    pstart = pend - padded
    cstart = jnp.cumsum(counts) - counts
    n_valid = (pend[-1] // tm).astype(jnp.int32)
    pos = (pstart[flat_e] + rank).astype(jnp.int32)

    tile_ids = jnp.arange(n_tiles, dtype=jnp.int32)
    te = jnp.searchsorted(pend, tile_ids * tm, side="right").astype(jnp.int32)
    te = jnp.minimum(te, n_exp - 1)
    te = jnp.where(tile_ids < n_valid, te, te[jnp.maximum(n_valid - 1, 0)])

    order = jnp.argsort(flat_e, stable=True).astype(jnp.int32)
    p = jnp.arange(n_tiles * tm, dtype=jnp.int32)
    e_p = jnp.repeat(te, tm)
    r = p - pstart[e_p]
    valid = (r < counts[e_p]) & (p < pend[-1])
    src_sorted = jnp.clip(cstart[e_p] + r, 0, a - 1)
    src_tok = jnp.where(valid, order[src_sorted] // TOP_K, 0).astype(jnp.int32)
    return src_tok, pos, te, n_valid.reshape(1)


def _moe(x, g, g_next, w_router, b_router, w_gu, b_gu, w_dn, b_dn, cfg):
    t, d = x.shape
    n_exp = w_router.shape[1]
    tm = cfg["moe_tm"]
    n_tiles = (t * TOP_K + n_exp * (tm - 1) + tm - 1) // tm
    idx, wgt, hp = _router(x, g, w_router, b_router, cfg["router_tm"])
    src_tok, pos, te, n_valid = _route_plan(idx[:, :TOP_K], n_exp, tm, n_tiles)
    xs = _dispatch(src_tok, hp, d, n_tiles, tm)
    act = _gmm_gu(te, n_valid, xs, w_gu, b_gu.reshape(n_exp, 1, -1), tm, cfg["moe_tn"])
    y = _gmm_dn(te, n_valid, act, w_dn, b_dn.reshape(n_exp, 1, -1), tm, cfg["moe_tn"])
    return _combine(pos, x, wgt, g_next, y, cfg["combine_rows"])


def _rope_tables(pos):
    freqs = ROPE_THETA ** (-jnp.arange(0, ROT_DIM, 2, dtype=F32) / ROT_DIM)
    ang = pos.astype(F32)[:, None] * freqs[None, :]
    cos, sin = jnp.cos(ang), jnp.sin(ang)
    t = pos.shape[0]
    rest = HEAD_DIM - ROT_DIM
    cos_tab = jnp.concatenate([cos, cos, jnp.ones((t, rest), F32)], axis=1)
    sin_tab = jnp.concatenate([-sin, sin, jnp.zeros((t, rest), F32)], axis=1)
    return cos_tab, sin_tab


def _mix_matrix():
    src = jnp.arange(HEAD_DIM)[:, None]
    dst = jnp.arange(HEAD_DIM)[None, :]
    perm = ((dst < ROT_HALF) & (src == dst + ROT_HALF)) | (
        (dst >= ROT_HALF) & (dst < ROT_DIM) & (src == dst - ROT_HALF))
    zero = jnp.zeros((HEAD_DIM, HEAD_DIM), F32)
    top = jnp.concatenate([jnp.ones((HEAD_DIM, HEAD_DIM), F32), zero], axis=1)
    bot = jnp.concatenate([zero, perm.astype(F32)], axis=1)
    return jnp.concatenate([top, bot], axis=0).astype(BF16)


def _forward(x_prompt, x_sample, cache_k, cache_v, norm1_g, norm2_g, w_in, q_norm_g, k_norm_g,
             lambda_qk, subln_g, gmlp_ln_g, gmlp_ln_b, gmlp_w_s, gmlp_b_s, gate_b, w_proj_a,
             w_proj_b, w_out, w_router, b_router, w_gu, b_gu, w_dn, b_dn):
    batch, seq, d = x_prompt.shape
    dec_batch, dec_seq, _ = x_sample.shape
    depth, _, past = cache_k.shape[:3]
    gw = gmlp_ln_g.shape[1]
    qk = (w_in.shape[2] - 2 * gw - 2 * d) // 3
    n_heads = qk // V_DIM
    tp, ts = batch * seq, dec_batch * dec_seq
    t_all = tp + ts
    assert tp % ts == 0 and seq % GMLP_CHUNK == 0 and past % GMLP_CHUNK == 0

    cfg = dict(
        tm=_pick(t_all, (1536, 512, 256, 128)),
        tm_p=_pick(tp, (1024, 512, 256, 128)),
        tm_s=_pick(ts, (512, 256, 128, 64)),
        tn=_pick(qk, (512, 256)),
        tn_d=_pick(d, (512, 256)),
        tq=_pick(seq, (1024, 512, 256, 128)),
        attn_pair=2,
        router_tm=_pick(t_all, (256, 128)),
        moe_tm=256 if t_all >= 4096 else 64,
        moe_tn=_pick(d, (1024, 512, 256)),
        combine_rows=64,
        gmlp_sub=_pick(seq // GMLP_CHUNK, (4, 2, 1)),
    )
    tm, tn = cfg["tm"], cfg["tn"]

    pos_all = jnp.concatenate([jnp.tile(jnp.arange(seq), batch),
                               jnp.tile(past + jnp.arange(dec_seq), dec_batch)])
    rope = _rope_tables(pos_all)
    x = jnp.concatenate([x_prompt.reshape(tp, d), x_sample.reshape(ts, d)], axis=0)
    cache_k2 = cache_k.reshape(depth * dec_batch * past, qk)
    cache_v2 = cache_v.reshape(depth * dec_batch * past, qk)

    h = _prenorm(x, norm1_g[0], _pick(t_all, (512, 256, 128)))
    tiles = qk // LANES
    kout = jnp.zeros((depth * tp * tiles, LANES), F32)
    vout = jnp.zeros((depth * tp * tiles, LANES), F32)
    ks, vs, gs = [], [], []
    for l in range(depth):
        lam_init = 0.8 - 0.6 * math.exp(-0.3 * l)
        g_eff = (subln_g[l] * (1.0 - lam_init)).reshape(1, V_DIM)
        q_bf, kt_bf, v_bf, k_s, v_s, u_act, vg_n, gates, kout, vout = _project_in(
            h, w_in[l].astype(BF16), q_norm_g[l], k_norm_g[l], rope,
            gmlp_ln_g[l], gmlp_ln_b[l], gate_b[l], (qk, gw), cfg, l, tp, kout, vout)

        score_bound = (HEAD_DIM * Q_SCALE * 1.01 * jnp.max(jnp.abs(q_norm_g[l]))
                       * jnp.max(jnp.abs(k_norm_g[l])))
        attend = functools.partial(_attend_prompt, q_bf, kt_bf, v_bf, lambda_qk[l], g_eff,
                                   lam_init, batch, seq, cfg["tq"], cfg["attn_pair"])
        o_p = lax.cond(score_bound <= SCORE_BOUND_LIMIT,
                       lambda: attend(False), lambda: attend(True))
        o_s = _attend_sample(q_bf, k_s, v_s, cache_k2, cache_v2, l, lambda_qk[l], g_eff,
                             lam_init, tp, dec_batch, dec_seq, past)
        att = jnp.concatenate([o_p, o_s], axis=0)

        gm_p = _gmlp_mix("gmlp_prompt", u_act, vg_n, gmlp_w_s[l], gmlp_b_s[l].T, 0, tp,
                         GMLP_CHUNK, cfg["gmlp_sub"])
        gm_s = _gmlp_mix("gmlp_sample", u_act, vg_n, gmlp_w_s[l][:, :dec_seq, :dec_seq],
                         gmlp_b_s[l][:, :dec_seq].T, tp, ts, dec_seq, dec_batch)
        gm = jnp.concatenate([gm_p, gm_s], axis=0)

        merged = _merge(att, gm, w_proj_a[l].astype(BF16), w_proj_b[l].astype(BF16), gates,
                        tm, cfg["tn_d"])
        x = _resid_matmul(x, merged, w_out[l].astype(BF16), tm, cfg["tn_d"])

        g_next = norm1_g[min(l + 1, depth - 1)]
        x, h = _moe(x, norm2_g[l], g_next, w_router[l], b_router[l], w_gu[l].astype(BF16),
                    b_gu[l], w_dn[l].astype(BF16), b_dn[l], cfg)

        ks.append(k_s.reshape(dec_batch, dec_seq, n_heads, 2, HEAD_DIM))
        vs.append(v_s.reshape(dec_batch, dec_seq, n_heads, V_DIM))
        gs.append(vg_n[tp:].reshape(dec_batch, dec_seq, gw))

    per_head = V_DIM // LANES
    new_k_prompt = kout.reshape(depth, batch, seq, n_heads, 2, HEAD_DIM)
    new_v_prompt = (vout.reshape(depth, batch, seq, per_head, n_heads, LANES)
                    .transpose(0, 1, 2, 4, 3, 5).reshape(depth, batch, seq, n_heads, V_DIM))
    return (x[:tp].reshape(batch, seq, d), x[tp:].reshape(dec_batch, dec_seq, d),
            new_k_prompt, new_v_prompt, jnp.stack(ks), jnp.stack(vs), jnp.stack(gs))


def kernel(x_prompt, x_sample, cache_k, cache_v, norm1_g, norm2_g, w_in, q_norm_g, k_norm_g, lambda_qk, subln_g, gmlp_ln_g, gmlp_ln_b, gmlp_w_s, gmlp_b_s, gate_b, w_proj_a, w_proj_b, w_out, w_router, b_router, w_gu, b_gu, w_dn, b_dn):
    return _forward(x_prompt, x_sample, cache_k, cache_v, norm1_g, norm2_g, w_in, q_norm_g,
                    k_norm_g, lambda_qk, subln_g, gmlp_ln_g, gmlp_ln_b, gmlp_w_s, gmlp_b_s,
                    gate_b, w_proj_a, w_proj_b, w_out, w_router, b_router, w_gu, b_gu, w_dn,
                    b_dn)
```

```python
import functools
import math

import jax
import jax.numpy as jnp
from jax import lax
from jax.experimental import pallas as pl
from jax.experimental.pallas import tpu as pltpu

F32 = jnp.float32
BF16 = jnp.bfloat16
U32 = jnp.uint32

LANES = 128
HEAD_DIM = 128
V_DIM = 2 * HEAD_DIM
CHUNK = 64
GMLP_CHUNK = 128
ROT_DIM = HEAD_DIM // 4
ROT_HALF = ROT_DIM // 2
ROPE_THETA = 500000.0
TOP_K = 4
SWIGLU_LIMIT = 7.0
SWIGLU_ALPHA = 1.702
EPS = 1e-6
NEG = -1e30
ATT_SCALE = 1.0 / math.sqrt(HEAD_DIM)
LOG2E = math.log2(math.e)
Q_SCALE = ATT_SCALE * LOG2E
GELU_C = math.sqrt(2.0 / math.pi)
SCORE_BOUND_LIMIT = 64.0
HI_MASK = 0xFFFF0000

VMEM_LIMIT = 52 * 1024 * 1024


def _params(*sem):
    return pltpu.CompilerParams(dimension_semantics=sem, vmem_limit_bytes=VMEM_LIMIT)


def _gelu(x):
    return 0.5 * x * (1.0 + jnp.tanh(GELU_C * (x + 0.044715 * (x * x * x))))


def _rms(x, g):
    return x * lax.rsqrt(jnp.mean(x * x, axis=-1, keepdims=True) + EPS) * g


def _pick(n, prefs):
    for p in prefs:
        if n % p == 0:
            return p
    return n


def _prenorm_kernel(x_ref, g_ref, o_ref):
    o_ref[...] = _rms(x_ref[...], g_ref[...]).astype(BF16)


def _prenorm(x, g, tm):
    t, d = x.shape
    return pl.pallas_call(
        _prenorm_kernel,
        grid=(t // tm,),
        in_specs=[pl.BlockSpec((tm, d), lambda i: (i, 0)),
                  pl.BlockSpec((1, d), lambda i: (0, 0))],
        out_specs=pl.BlockSpec((tm, d), lambda i: (i, 0)),
        out_shape=jax.ShapeDtypeStruct((t, d), BF16),
        compiler_params=_params("parallel"),
        name="prenorm",
    )(x, g.reshape(1, d))


def _mm_kernel(*refs, epilogue, n_extra, n_alias):
    h_ref, w_ref = refs[:2]
    extra = refs[2:2 + n_extra]
    outs = refs[2 + n_extra + n_alias:]
    z = jnp.dot(h_ref[...], w_ref[...], preferred_element_type=F32)
    epilogue(z, pl.program_id(1), extra, outs)


def _matmul_epi(name, h, w, col0, ncols, tm, tn, epilogue, extras, extra_specs,
                out_shapes, out_specs, row0=0, rows=None, aliased=(), wrow=0):
    t, d = h.shape
    rows = t - row0 if rows is None else rows
    assert rows % tm == 0 and row0 % tm == 0 and ncols % tn == 0 and col0 % tn == 0
    rb0, jb = row0 // tm, col0 // tn
    in_specs = [
        pl.BlockSpec((tm, d), lambda i, j: (rb0 + i, 0)),
        pl.BlockSpec((d, tn), lambda i, j: (wrow, jb + j)),
    ] + list(extra_specs) + [pl.BlockSpec(memory_space=pl.ANY)] * len(aliased)
    n_in = 2 + len(extras)
    kern = functools.partial(_mm_kernel, epilogue=epilogue, n_extra=len(extras),
                             n_alias=len(aliased))
    return pl.pallas_call(
        kern,
        grid=(rows // tm, ncols // tn),
        in_specs=in_specs,
        out_specs=out_specs,
        out_shape=out_shapes,
        input_output_aliases={n_in + a: a for a in range(len(aliased))},
        compiler_params=_params("parallel", "arbitrary"),
        name=name,
    )(h, w, *extras, *aliased)


def _headnorm_rope(zc, mix, ct, st):
    lhs = jnp.concatenate([(zc * zc).astype(BF16), zc.astype(BF16)], axis=1)
    red = jnp.dot(lhs, mix, preferred_element_type=F32)
    rstd = lax.rsqrt(red[:, :HEAD_DIM] * (1.0 / HEAD_DIM) + EPS)
    return (zc * ct + red[:, HEAD_DIM:] * st) * rstd


def _head_slices(tn):
    return [slice(h * HEAD_DIM, (h + 1) * HEAD_DIM) for h in range(tn // HEAD_DIM)]


def _store_token_major(out_ref, j, n_tiles, pieces, tiles, dest):
    tm = pieces[0].shape[0]
    for jj in range(n_tiles):
        @pl.when(j == jj)
        def _(jj=jj):
            for hh, r in enumerate(pieces):
                out_ref[pl.ds(dest(jj * len(pieces) + hh), tm, stride=tiles), :] = r


def _epi_q(z, j, extra, outs, *, tn):
    mix, ct, st = (r[...] for r in extra)
    for sl in _head_slices(tn):
        outs[0][:, sl] = _headnorm_rope(z[:, sl], mix, ct, st).astype(BF16)


def _epi_k_prompt(z, j, extra, outs, *, tn, n_tiles):
    mix, ct, st = (r[...] for r in extra)
    kout_ref, kt_ref = outs
    pieces = []
    for sl in _head_slices(tn):
        r = _headnorm_rope(z[:, sl], mix, ct, st)
        kt_ref[sl, :] = r.T.astype(BF16)
        pieces.append(r)
    _store_token_major(kout_ref, j, n_tiles, pieces, n_tiles * len(pieces), lambda c: c)


def _epi_k_sample(z, j, extra, outs, *, tn):
    mix, ct, st = (r[...] for r in extra)
    for sl in _head_slices(tn):
        outs[0][:, sl] = _headnorm_rope(z[:, sl], mix, ct, st)


def _epi_v_prompt(z, j, extra, outs, *, tn, n_tiles):
    vout_ref, vbf_ref = outs
    vbf_ref[...] = z.astype(BF16)
    pieces = [z[:, sl] for sl in _head_slices(tn)]
    tiles = n_tiles * len(pieces)
    per_head = V_DIM // LANES
    _store_token_major(vout_ref, j, n_tiles, pieces, tiles,
                       lambda c: (c % per_head) * (tiles // per_head) + c // per_head)


def _epi_copy(z, j, extra, outs):
    for o in outs:
        o[...] = z.astype(o.dtype)


def _epi_gelu(z, j, extra, outs):
    outs[0][...] = _gelu(z).astype(outs[0].dtype)


def _epi_gelu_layernorm(z, j, extra, outs, *, tn, n_tiles):
    o = outs[0]
    a = _gelu(z)
    for jj in range(n_tiles):
        @pl.when(j == jj)
        def _(jj=jj):
            o[:, jj * tn:(jj + 1) * tn] = a

    @pl.when(j == n_tiles - 1)
    def _():
        full = o[...]
        mu = jnp.mean(full, axis=-1, keepdims=True)
        cen = full - mu
        var = jnp.mean(cen * cen, axis=-1, keepdims=True)
        o[...] = cen * lax.rsqrt(var + EPS) * extra[0][...] + extra[1][...]


def _epi_sigmoid_bias(z, j, extra, outs):
    outs[0][...] = jax.nn.sigmoid(z + extra[0][...]).astype(outs[0].dtype)


def _project_in(h, w_in, qg, kg, rope, ln_g, ln_b, gate_b, dims, cfg, layer, tp, kout, vout):
    t, d = h.shape
    ts = t - tp
    qk, gw = dims
    tm, tn, tm_p, tm_s = cfg["tm"], cfg["tn"], cfg["tm_p"], cfg["tm_s"]
    cos_tab, sin_tab = rope
    o1, o2, o3, o4, o5 = qk, 2 * qk, 3 * qk, 3 * qk + gw, 3 * qk + 2 * gw
    tiles = qk // LANES
    n_tiles = qk // tn
    vec = lambda n: pl.BlockSpec((1, n), lambda i, j: (0, 0))
    out_tile = lambda tmx: pl.BlockSpec((tmx, tn), lambda i, j: (i, j))
    sds = jax.ShapeDtypeStruct
    mix = _mix_matrix()

    def rope_specs(tmx, row0):
        rb0 = row0 // tmx
        row_tile = pl.BlockSpec((tmx, HEAD_DIM), lambda i, j: (rb0 + i, 0))
        return [pl.BlockSpec((2 * HEAD_DIM, 2 * HEAD_DIM), lambda i, j: (0, 0)),
                row_tile, row_tile]

    def gain_tables(gain, scale):
        lane = jnp.arange(HEAD_DIM)
        partner = jnp.where(lane < ROT_HALF, jnp.roll(gain, -ROT_HALF), jnp.roll(gain, ROT_HALF))
        return cos_tab * (gain * scale)[None, :], sin_tab * (partner * scale)[None, :]

    stacked = lambda: pl.BlockSpec((tm_p * tiles, LANES),
                                   lambda i, j: (layer * (tp // tm_p) + i, 0))
    k_tabs = gain_tables(kg, 1.0)

    (q_bf,) = _matmul_epi(
        "proj_q", h, w_in, 0, qk, tm, tn, functools.partial(_epi_q, tn=tn),
        [mix, *gain_tables(qg, Q_SCALE)], rope_specs(tm, 0),
        [sds((t, qk), BF16)], [out_tile(tm)], wrow=layer)
    kout, kt_bf = _matmul_epi(
        "proj_k_prompt", h, w_in, o1, qk, tm_p, tn,
        functools.partial(_epi_k_prompt, tn=tn, n_tiles=n_tiles),
        [mix, *k_tabs], rope_specs(tm_p, 0),
        [sds(kout.shape, F32), sds((qk, tp), BF16)],
        [stacked(), pl.BlockSpec((tn, tm_p), lambda i, j: (j, i))],
        rows=tp, aliased=(kout,), wrow=layer)
    (k_s,) = _matmul_epi(
        "proj_k_sample", h, w_in, o1, qk, tm_s, tn, functools.partial(_epi_k_sample, tn=tn),
        [mix, *k_tabs], rope_specs(tm_s, tp),
        [sds((ts, qk), F32)], [out_tile(tm_s)], row0=tp, wrow=layer)
    vout, v_bf = _matmul_epi(
        "proj_v_prompt", h, w_in, o2, qk, tm_p, tn,
        functools.partial(_epi_v_prompt, tn=tn, n_tiles=n_tiles), [], [],
        [sds(vout.shape, F32), sds((tp, qk), BF16)], [stacked(), out_tile(tm_p)],
        rows=tp, aliased=(vout,), wrow=layer)
    (v_s,) = _matmul_epi(
        "proj_v_sample", h, w_in, o2, qk, tm_s, tn, _epi_copy, [], [],
        [sds((ts, qk), F32)], [out_tile(tm_s)], row0=tp, wrow=layer)
    (u_act,) = _matmul_epi(
        "proj_u", h, w_in, o3, gw, tm, tn, _epi_gelu, [], [],
        [sds((t, gw), BF16)], [out_tile(tm)], wrow=layer)
    (vg_n,) = _matmul_epi(
        "proj_vg", h, w_in, o4, gw, tm, tn,
        functools.partial(_epi_gelu_layernorm, tn=tn, n_tiles=gw // tn),
        [ln_g.reshape(1, gw), ln_b.reshape(1, gw)], [vec(gw), vec(gw)],
        [sds((t, gw), F32)], [pl.BlockSpec((tm, gw), lambda i, j: (i, 0))], wrow=layer)
    (gates,) = _matmul_epi(
        "proj_gates", h, w_in, o5, 2 * d, tm, tn, _epi_sigmoid_bias,
        [gate_b.reshape(1, 2 * d)], [pl.BlockSpec((1, tn), lambda i, j: (0, j))],
        [sds((t, 2 * d), BF16)], [out_tile(tm)], wrow=layer)
    return q_bf, kt_bf, v_bf, k_s, v_s, u_act, vg_n, gates, kout, vout


def _lambda(lq_ref, lam_init):
    lq = lq_ref[...]
    a = jnp.sum(lq[0:1] * lq[1:2], axis=-1, keepdims=True)
    b = jnp.sum(lq[2:3] * lq[3:4], axis=-1, keepdims=True)
    return jnp.exp(a) - jnp.exp(b) + lam_init


def _attn_finish(lam, g, a1, l1, a2, l2):
    o = a1 / l1 - lam * (a2 / l2)
    return _rms(o, g).astype(BF16)


def _attn_prompt_kernel(lq_ref, g_ref, q_ref, kt_ref, v_ref, o_ref, *scratch,
                        tq, lam_init, pair, running_max):
    qi = pl.program_id(2)
    half = tq // 2
    state = (scratch[0::2], scratch[1::2])

    def block(off, width, r0, diag_col0):
        rows = tq - r0
        v = v_ref[pl.ds(off, width), :]
        for mp in range(2):
            a_ref, l_ref = state[mp][:2]
            q = q_ref[r0:, mp * HEAD_DIM:(mp + 1) * HEAD_DIM]
            kt = kt_ref[mp * HEAD_DIM:(mp + 1) * HEAD_DIM, pl.ds(off, width)]
            s = jnp.dot(q, kt, preferred_element_type=F32)
            vis = None
            if diag_col0 is not None:
                rc = (lax.broadcasted_iota(jnp.int32, (rows, width), 0) + r0) // CHUNK
                cc = (lax.broadcasted_iota(jnp.int32, (rows, width), 1) + diag_col0) // CHUNK
                vis = cc <= rc
            if running_max:
                m_ref = state[mp][2]
                if vis is not None:
                    s = jnp.where(vis, s, NEG)
                m_old = m_ref[r0:, :]
                m_new = jnp.maximum(m_old, jnp.max(s, axis=-1, keepdims=True))
                alpha = jnp.exp2(m_old - m_new)
                p = jnp.exp2(s - m_new)
                m_ref[r0:, :] = m_new
                l_ref[r0:, :] = alpha * l_ref[r0:, :] + jnp.sum(p, axis=-1, keepdims=True)
                a_ref[r0:, :] = alpha * a_ref[r0:, :] + jnp.dot(
                    p.astype(BF16), v, preferred_element_type=F32)
            else:
                p = jnp.exp2(s)
                if vis is not None:
                    p = jnp.where(vis, p, 0.0)
                l_ref[r0:, :] += jnp.sum(p, axis=-1, keepdims=True)
                a_ref[r0:, :] += jnp.dot(p.astype(BF16), v, preferred_element_type=F32)

    for refs in state:
        refs[0][...] = jnp.zeros(refs[0].shape, F32)
        refs[1][...] = jnp.zeros(refs[1].shape, F32)
        if running_max:
            refs[2][...] = jnp.full(refs[2].shape, NEG, F32)

    def body(j, _):
        off = pl.multiple_of(j * (pair * tq), pair * tq)
        for u in range(pair):
            block(off + u * tq, tq, 0, None)
        return 0

    lax.fori_loop(0, qi // pair, body, 0)
    for u in range(pair - 1):
        @pl.when(qi % pair > u)
        def _(u=u):
            block(pl.multiple_of((qi - 1 - u) * tq, tq), tq, 0, None)

    base = pl.multiple_of(qi * tq, tq)
    block(base, half, 0, 0)
    block(base + half, half, half, half)
    (a1, l1), (a2, l2) = (r[:2] for r in state)
    o_ref[...] = _attn_finish(_lambda(lq_ref, lam_init), g_ref[...], a1[...], l1[...],
                              a2[...], l2[...])


def _attend_prompt(q_bf, kt_bf, v_bf, lq, g_eff, lam_init, batch, seq, tq, pair, running_max):
    qk = q_bf.shape[1]
    n_heads = qk // V_DIM
    nq = seq // tq
    kern = functools.partial(_attn_prompt_kernel, tq=tq, lam_init=lam_init, pair=pair,
                             running_max=running_max)
    per_map = [pltpu.VMEM((tq, V_DIM), F32), pltpu.VMEM((tq, 1), F32)]
    if running_max:
        per_map.append(pltpu.VMEM((tq, 1), F32))
    scratch = [s for s in per_map for _ in range(2)]
    return pl.pallas_call(
        kern,
        grid=(batch, n_heads, nq),
        in_specs=[
            pl.BlockSpec((4, HEAD_DIM), lambda b, h, i: (0, 0)),
            pl.BlockSpec((1, V_DIM), lambda b, h, i: (0, 0)),
            pl.BlockSpec((tq, V_DIM), lambda b, h, i: (b * nq + i, h)),
            pl.BlockSpec((V_DIM, seq), lambda b, h, i: (h, b)),
            pl.BlockSpec((seq, V_DIM), lambda b, h, i: (b, h)),
        ],
        out_specs=pl.BlockSpec((tq, V_DIM), lambda b, h, i: (b * nq + i, h)),
        out_shape=jax.ShapeDtypeStruct((batch * seq, qk), BF16),
        scratch_shapes=scratch,
        compiler_params=_params("parallel", "parallel", "arbitrary"),
        name="attn_prompt_max" if running_max else "attn_prompt",
    )(lq, g_eff, q_bf, kt_bf, v_bf)


def _attn_sample_kernel(lq_ref, g_ref, q_ref, kc_ref, vc_ref, kn_ref, vn_ref, o_ref, *,
                        lam_init):
    q = q_ref[...]
    kc = kc_ref[...].astype(BF16)
    vc = vc_ref[...].astype(BF16)
    kn = kn_ref[...].astype(BF16)
    vn = vn_ref[...].astype(BF16)
    res = []
    for mp in range(2):
        sl = slice(mp * HEAD_DIM, (mp + 1) * HEAD_DIM)
        sc = lax.dot_general(q[:, sl], kc[:, sl], (((1,), (1,)), ((), ())),
                             preferred_element_type=F32)
        sn = lax.dot_general(q[:, sl], kn[:, sl], (((1,), (1,)), ((), ())),
                             preferred_element_type=F32)
        m = jnp.maximum(jnp.max(sc, axis=-1, keepdims=True),
                        jnp.max(sn, axis=-1, keepdims=True))
        pc = jnp.exp2(sc - m)
        pn = jnp.exp2(sn - m)
        l = jnp.sum(pc, axis=-1, keepdims=True) + jnp.sum(pn, axis=-1, keepdims=True)
        acc = (jnp.dot(pc.astype(BF16), vc, preferred_element_type=F32)
               + jnp.dot(pn.astype(BF16), vn, preferred_element_type=F32))
        res += [acc, l]
    o_ref[...] = _attn_finish(_lambda(lq_ref, lam_init), g_ref[...], *res)


def _attend_sample(q_bf, k_s, v_s, cache_k2, cache_v2, layer, lq, g_eff, lam_init,
                   row0, dec_batch, dec_seq, past):
    qk = q_bf.shape[1]
    n_heads = qk // V_DIM
    rb0 = row0 // dec_seq
    q_rows = pl.BlockSpec((dec_seq, V_DIM), lambda b, h: (rb0 + b, h))
    new_rows = pl.BlockSpec((dec_seq, V_DIM), lambda b, h: (b, h))
    cache_rows = pl.BlockSpec((past, V_DIM), lambda b, h: (layer * dec_batch + b, h))
    kern = functools.partial(_attn_sample_kernel, lam_init=lam_init)
    return pl.pallas_call(
        kern,
        grid=(dec_batch, n_heads),
        in_specs=[
            pl.BlockSpec((4, HEAD_DIM), lambda b, h: (0, 0)),
            pl.BlockSpec((1, V_DIM), lambda b, h: (0, 0)),
            q_rows, cache_rows, cache_rows, new_rows, new_rows,
        ],
        out_specs=pl.BlockSpec((dec_seq, V_DIM), lambda b, h: (b, h)),
        out_shape=jax.ShapeDtypeStruct((dec_batch * dec_seq, qk), BF16),
        compiler_params=_params("parallel", "parallel"),
        name="attn_sample",
    )(lq, g_eff, q_bf, cache_k2, cache_v2, k_s, v_s)


def _gmlp_kernel(w_ref, b_ref, u_ref, v_ref, o_ref, *, c, n_sub, groups):
    gd = u_ref.shape[1] // groups
    tri = (lax.broadcasted_iota(jnp.int32, (c, c), 0)
           >= lax.broadcasted_iota(jnp.int32, (c, c), 1))
    for g in range(groups):
        wg = jnp.where(tri, w_ref[g], 0.0).astype(BF16)
        bg = b_ref[:, g:g + 1]
        cs = slice(g * gd, (g + 1) * gd)
        for s in range(n_sub):
            rs = slice(s * c, (s + 1) * c)
            mixed = jnp.dot(wg, v_ref[rs, cs].astype(BF16), preferred_element_type=F32) + bg
            o_ref[rs, cs] = (u_ref[rs, cs].astype(F32) * mixed).astype(BF16)


def _gmlp_mix(name, u, vg, w_s, b_s_t, row0, rows, c, n_sub):
    groups = w_s.shape[0]
    gw = u.shape[1]
    tr = c * n_sub
    rb0 = row0 // tr
    rows_spec = pl.BlockSpec((tr, gw), lambda i: (rb0 + i, 0))
    kern = functools.partial(_gmlp_kernel, c=c, n_sub=n_sub, groups=groups)
    return pl.pallas_call(
        kern,
        grid=(rows // tr,),
        in_specs=[
            pl.BlockSpec((groups, c, c), lambda i: (0, 0, 0)),
            pl.BlockSpec((c, groups), lambda i: (0, 0)),
            rows_spec, rows_spec,
        ],
        out_specs=pl.BlockSpec((tr, gw), lambda i: (i, 0)),
        out_shape=jax.ShapeDtypeStruct((rows, gw), BF16),
        compiler_params=_params("parallel"),
        name=name,
    )(w_s, b_s_t, u, vg)


def _merge_kernel(a_ref, m_ref, wa_ref, wb_ref, g0_ref, g1_ref, o_ref):
    ya = jnp.dot(a_ref[...], wa_ref[...], preferred_element_type=F32)
    yb = jnp.dot(m_ref[...], wb_ref[...], preferred_element_type=F32)
    o_ref[...] = (g0_ref[...].astype(F32) * ya + g1_ref[...].astype(F32) * yb).astype(BF16)


def _merge(att, gm, w_pa, w_pb, layer, gates, tm, tn):
    t, wa = att.shape
    d = w_pa.shape[1]
    nj = d // tn
    return pl.pallas_call(
        _merge_kernel,
        grid=(t // tm, nj),
        in_specs=[
            pl.BlockSpec((tm, wa), lambda i, j: (i, 0)),
            pl.BlockSpec((tm, gm.shape[1]), lambda i, j: (i, 0)),
            pl.BlockSpec((wa, tn), lambda i, j: (layer, j)),
            pl.BlockSpec((gm.shape[1], tn), lambda i, j: (layer, j)),
            pl.BlockSpec((tm, tn), lambda i, j: (i, j)),
            pl.BlockSpec((tm, tn), lambda i, j: (i, nj + j)),
        ],
        out_specs=pl.BlockSpec((tm, tn), lambda i, j: (i, j)),
        out_shape=jax.ShapeDtypeStruct((t, d), BF16),
        compiler_params=_params("parallel", "arbitrary"),
        name="merge",
    )(att, gm, w_pa, w_pb, gates, gates)


def _resid_matmul_kernel(x_ref, m_ref, w_ref, o_ref):
    o_ref[...] = x_ref[...] + jnp.dot(m_ref[...], w_ref[...], preferred_element_type=F32)


def _resid_matmul(x, m, w, layer, tm, tn):
    t, d = x.shape
    return pl.pallas_call(
        _resid_matmul_kernel,
        grid=(t // tm, d // tn),
        in_specs=[
            pl.BlockSpec((tm, tn), lambda i, j: (i, j)),
            pl.BlockSpec((tm, m.shape[1]), lambda i, j: (i, 0)),
            pl.BlockSpec((m.shape[1], tn), lambda i, j: (layer, j)),
        ],
        out_specs=pl.BlockSpec((tm, tn), lambda i, j: (i, j)),
        out_shape=jax.ShapeDtypeStruct((t, d), F32),
        compiler_params=_params("parallel", "arbitrary"),
        name="out_proj",
    )(x, m, w)


def _router_kernel(x_ref, g_ref, wr_ref, br_ref, idx_ref, wgt_ref, hp_ref):
    tm, d = x_ref.shape
    h = _rms(x_ref[...], g_ref[...])
    logits = jnp.dot(h, wr_ref[...], preferred_element_type=F32,
                     precision=lax.Precision.HIGHEST) + br_ref[...]
    n_exp = logits.shape[1]
    lane_e = lax.broadcasted_iota(jnp.int32, logits.shape, 1)
    lane_o = lax.broadcasted_iota(jnp.int32, idx_ref.shape, 1)
    idx_out = jnp.zeros(idx_ref.shape, jnp.int32)
    val_out = jnp.full(wgt_ref.shape, -jnp.inf, F32)
    top = None
    for k in range(TOP_K):
        m = jnp.max(logits, axis=-1, keepdims=True)
        i = jnp.min(jnp.where(logits == m, lane_e, n_exp), axis=-1, keepdims=True)
        logits = jnp.where(lane_e == i, -jnp.inf, logits)
        idx_out = jnp.where(lane_o == k, i, idx_out)
        val_out = jnp.where(lane_o == k, m, val_out)
        if k == 0:
            top = m
    e = jnp.exp(val_out - top)
    idx_ref[...] = idx_out
    wgt_ref[...] = e / jnp.sum(e, axis=-1, keepdims=True)

    nw = d // (2 * LANES)
    bits = lax.bitcast_convert_type(h.astype(BF16).astype(F32), U32)
    for c in range(nw):
        lo = bits[:, c * LANES:(c + 1) * LANES] >> 16
        hi = bits[:, (c + nw) * LANES:(c + nw + 1) * LANES] & jnp.uint32(HI_MASK)
        hp_ref[pl.ds(c, tm, stride=nw), :] = lo | hi


def _router(x, g, w_router, b_router, tm):
    t, d = x.shape
    n_exp = w_router.shape[1]
    nw = d // (2 * LANES)
    return pl.pallas_call(
        _router_kernel,
        grid=(t // tm,),
        in_specs=[
            pl.BlockSpec((tm, d), lambda i: (i, 0)),
            pl.BlockSpec((1, d), lambda i: (0, 0)),
            pl.BlockSpec((d, n_exp), lambda i: (0, 0)),
            pl.BlockSpec((1, n_exp), lambda i: (0, 0)),
        ],
        out_specs=[pl.BlockSpec((tm, LANES), lambda i: (i, 0)),
                   pl.BlockSpec((tm, LANES), lambda i: (i, 0)),
                   pl.BlockSpec((tm * nw, LANES), lambda i: (i, 0))],
        out_shape=[jax.ShapeDtypeStruct((t, LANES), jnp.int32),
                   jax.ShapeDtypeStruct((t, LANES), F32),
                   jax.ShapeDtypeStruct((t * nw, LANES), U32)],
        compiler_params=_params("parallel"),
        name="router",
    )(x, g.reshape(1, d), w_router, b_router.reshape(1, n_exp))


ISSUE_UNROLL = 8


def _dispatch_kernel(src_ref, hp_hbm, o_ref, buf, sem, *, rows, nw):
    i = pl.program_id(0)
    n = pl.num_programs(0)

    def row_copy(tok_row, slot, r):
        return pltpu.make_async_copy(hp_hbm.at[pl.ds(tok_row, nw), :],
                                     buf.at[slot, pl.ds(r * nw, nw), :], sem.at[slot])

    def issue(tile, slot):
        def body(rb, _):
            for u in range(ISSUE_UNROLL):
                r = rb * ISSUE_UNROLL + u
                tok_row = pl.multiple_of(src_ref[tile * rows + r] * nw, nw)
                row_copy(tok_row, slot, r).start(priority=u % 2)
            return 0
        lax.fori_loop(0, rows // ISSUE_UNROLL, body, 0)

    slot = i % 2

    @pl.when(i == 0)
    def _():
        issue(0, 0)

    @pl.when(i + 1 < n)
    def _():
        issue(i + 1, 1 - slot)

    for r in range(rows):
        row_copy(0, slot, r).wait()

    los, his = [], []
    for c in range(nw):
        w = buf[slot, pl.ds(c, rows, stride=nw), :]
        los.append(lax.bitcast_convert_type(w << 16, F32))
        his.append(lax.bitcast_convert_type(w & jnp.uint32(HI_MASK), F32))
    o_ref[...] = jnp.concatenate(los + his, axis=1).astype(BF16)


def _dispatch(src_tok, hp, d, n_tiles, rows):
    nw = d // (2 * LANES)
    kern = functools.partial(_dispatch_kernel, rows=rows, nw=nw)
    return pl.pallas_call(
        kern,
        grid_spec=pltpu.PrefetchScalarGridSpec(
            num_scalar_prefetch=1,
            grid=(n_tiles,),
            in_specs=[pl.BlockSpec(memory_space=pl.ANY)],
            out_specs=pl.BlockSpec((rows, d), lambda i, s: (i, 0)),
            scratch_shapes=[pltpu.VMEM((2, rows * nw, LANES), U32),
                            pltpu.SemaphoreType.DMA((2,))],
        ),
        out_shape=jax.ShapeDtypeStruct((n_tiles * rows, d), BF16),
        compiler_params=_params("arbitrary"),
        name="dispatch",
    )(src_tok, hp)


def _new_expert(te_ref, i):
    return (i == 0) | (te_ref[i] != te_ref[jnp.maximum(i - 1, 0)])


def _gmm_gu_kernel(te_ref, nt_ref, x_ref, wg_ref, wu_ref, bg_ref, bu_ref, o_ref, wg_bf, wu_bf):
    i = pl.program_id(1)

    @pl.when(_new_expert(te_ref, i))
    def _():
        wg_bf[...] = wg_ref[0].astype(BF16)
        wu_bf[...] = wu_ref[0].astype(BF16)

    @pl.when(i < nt_ref[0])
    def _():
        x = x_ref[...]
        gate = jnp.dot(x, wg_bf[...], preferred_element_type=F32) + bg_ref[0]
        up = jnp.dot(x, wu_bf[...], preferred_element_type=F32) + bu_ref[0]
        gate = jnp.minimum(gate, SWIGLU_LIMIT)
        up = jnp.clip(up, -SWIGLU_LIMIT, SWIGLU_LIMIT)
        act = (up + 1.0) * (gate * jax.nn.sigmoid(SWIGLU_ALPHA * gate))
        o_ref[...] = act.astype(BF16)

    @pl.when(i >= nt_ref[0])
    def _():
        o_ref[...] = jnp.zeros(o_ref.shape, BF16)


def _gmm_gu(tile_exp, n_valid, xs, w_gu, b_gu3, tm, tn):
    p, d = xs.shape
    f = w_gu.shape[2] // 2
    nj = f // tn
    return pl.pallas_call(
        _gmm_gu_kernel,
        grid_spec=pltpu.PrefetchScalarGridSpec(
            num_scalar_prefetch=2,
            grid=(nj, p // tm),
            in_specs=[
                pl.BlockSpec((tm, d), lambda j, i, te, nt: (i, 0)),
                pl.BlockSpec((1, d, tn), lambda j, i, te, nt: (te[i], 0, j)),
                pl.BlockSpec((1, d, tn), lambda j, i, te, nt: (te[i], 0, nj + j)),
                pl.BlockSpec((1, 1, tn), lambda j, i, te, nt: (te[i], 0, j)),
                pl.BlockSpec((1, 1, tn), lambda j, i, te, nt: (te[i], 0, nj + j)),
            ],
            out_specs=pl.BlockSpec((tm, tn), lambda j, i, te, nt: (i, j)),
            scratch_shapes=[pltpu.VMEM((d, tn), BF16), pltpu.VMEM((d, tn), BF16)],
        ),
        out_shape=jax.ShapeDtypeStruct((p, f), BF16),
        compiler_params=_params("arbitrary", "arbitrary"),
        name="expert_gate_up",
    )(tile_exp, n_valid, xs, w_gu, w_gu, b_gu3, b_gu3)


def _gmm_dn_kernel(te_ref, nt_ref, a_ref, w_ref, b_ref, o_ref, w_bf):
    i = pl.program_id(1)

    @pl.when(_new_expert(te_ref, i))
    def _():
        w_bf[...] = w_ref[0].astype(BF16)

    @pl.when(i < nt_ref[0])
    def _():
        o_ref[...] = jnp.dot(a_ref[...], w_bf[...], preferred_element_type=F32) + b_ref[0]

    @pl.when(i >= nt_ref[0])
    def _():
        o_ref[...] = jnp.zeros(o_ref.shape, F32)


def _gmm_dn(tile_exp, n_valid, act, w_dn, b_dn3, tm, tn):
    p, f = act.shape
    d = w_dn.shape[2]
    return pl.pallas_call(
        _gmm_dn_kernel,
        grid_spec=pltpu.PrefetchScalarGridSpec(
            num_scalar_prefetch=2,
            grid=(d // tn, p // tm),
            in_specs=[
                pl.BlockSpec((tm, f), lambda j, i, te, nt: (i, 0)),
                pl.BlockSpec((1, f, tn), lambda j, i, te, nt: (te[i], 0, j)),
                pl.BlockSpec((1, 1, tn), lambda j, i, te, nt: (te[i], 0, j)),
            ],
            out_specs=pl.BlockSpec((tm, tn), lambda j, i, te, nt: (i, j)),
            scratch_shapes=[pltpu.VMEM((f, tn), BF16)],
        ),
        out_shape=jax.ShapeDtypeStruct((p, d), F32),
        compiler_params=_params("arbitrary", "arbitrary"),
        name="expert_down",
    )(tile_exp, n_valid, act, w_dn, b_dn3)


def _combine_kernel(pos_ref, x_ref, w_ref, g_ref, y_hbm, o_ref, h_ref, buf, sem, *, rows):
    i = pl.program_id(0)
    n = pl.num_programs(0)

    def row_copy(src_row, slot, k, r):
        return pltpu.make_async_copy(y_hbm.at[pl.ds(src_row, 1), :],
                                     buf.at[slot, k, pl.ds(r, 1), :], sem.at[slot])

    def issue(tile, slot):
        def body(rb, _):
            for u in range(ISSUE_UNROLL // TOP_K):
                r = rb * (ISSUE_UNROLL // TOP_K) + u
                base = (tile * rows + r) * TOP_K
                for k in range(TOP_K):
                    row_copy(pos_ref[base + k], slot, k, r).start(priority=k % 2)
            return 0
        lax.fori_loop(0, rows * TOP_K // ISSUE_UNROLL, body, 0)

    slot = i % 2

    @pl.when(i == 0)
    def _():
        issue(0, 0)

    @pl.when(i + 1 < n)
    def _():
        issue(i + 1, 1 - slot)

    for r in range(rows):
        for k in range(TOP_K):
            row_copy(0, slot, k, r).wait()

    w = w_ref[...]
    acc = x_ref[...]
    for k in range(TOP_K):
        acc = acc + w[:, k:k + 1] * buf[slot, k]
    o_ref[...] = acc
    h_ref[...] = _rms(acc, g_ref[...]).astype(BF16)


def _combine(pos, x, wgt, g_next, y, rows):
    t, d = x.shape
    kern = functools.partial(_combine_kernel, rows=rows)
    return pl.pallas_call(
        kern,
        grid_spec=pltpu.PrefetchScalarGridSpec(
            num_scalar_prefetch=1,
            grid=(t // rows,),
            in_specs=[
                pl.BlockSpec((rows, d), lambda i, s: (i, 0)),
                pl.BlockSpec((rows, wgt.shape[1]), lambda i, s: (i, 0)),
                pl.BlockSpec((1, d), lambda i, s: (0, 0)),
                pl.BlockSpec(memory_space=pl.ANY),
            ],
            out_specs=[pl.BlockSpec((rows, d), lambda i, s: (i, 0)),
                       pl.BlockSpec((rows, d), lambda i, s: (i, 0))],
            scratch_shapes=[pltpu.VMEM((2, TOP_K, rows, d), F32),
                            pltpu.SemaphoreType.DMA((2,))],
        ),
        out_shape=[jax.ShapeDtypeStruct((t, d), F32), jax.ShapeDtypeStruct((t, d), BF16)],
        compiler_params=_params("arbitrary"),
        name="combine",
    )(pos, x, wgt, g_next.reshape(1, d), y)


def _route_plan(top_i, n_exp, tm, n_tiles):
    t = top_i.shape[0]
    a = t * TOP_K
    flat_e = top_i.reshape(a)
    onehot = (flat_e[:, None] == jnp.arange(n_exp, dtype=jnp.int32)[None, :]).astype(jnp.int32)
    csum = jnp.cumsum(onehot, axis=0)
    counts = csum[-1]
    rank = jnp.take_along_axis(csum, flat_e[:, None], axis=1)[:, 0] - 1
    padded = ((counts + tm - 1) // tm) * tm
    pend = jnp.cumsum(padded)
    pstart = pend - padded
    cstart = jnp.cumsum(counts) - counts
    n_valid = (pend[-1] // tm).astype(jnp.int32)
    pos = (pstart[flat_e] + rank).astype(jnp.int32)

    tile_ids = jnp.arange(n_tiles, dtype=jnp.int32)
    te = jnp.searchsorted(pend, tile_ids * tm, side="right").astype(jnp.int32)
    te = jnp.minimum(te, n_exp - 1)
    te = jnp.where(tile_ids < n_valid, te, te[jnp.maximum(n_valid - 1, 0)])

    order = jnp.argsort(flat_e, stable=True).astype(jnp.int32)
    p = jnp.arange(n_tiles * tm, dtype=jnp.int32)
    e_p = jnp.repeat(te, tm)
    r = p - pstart[e_p]
    valid = (r < counts[e_p]) & (p < pend[-1])
    src_sorted = jnp.clip(cstart[e_p] + r, 0, a - 1)
    src_tok = jnp.where(valid, order[src_sorted] // TOP_K, 0).astype(jnp.int32)
    return src_tok, pos, te, n_valid.reshape(1)


def _moe(x, g, g_next, w_router, b_router, w_gu, b_gu, w_dn, b_dn, layer, cfg):
    t, d = x.shape
    n_exp = w_router.shape[1]
    tm = cfg["moe_tm"]
    n_tiles = (t * TOP_K + n_exp * (tm - 1) + tm - 1) // tm
    idx, wgt, hp = _router(x, g, w_router, b_router, cfg["router_tm"])
    src_tok, pos, te, n_valid = _route_plan(idx[:, :TOP_K], n_exp, tm, n_tiles)
    te = te + layer * n_exp
    xs = _dispatch(src_tok, hp, d, n_tiles, tm)
    act = _gmm_gu(te, n_valid, xs, w_gu, b_gu, tm, cfg["moe_tn"])
    y = _gmm_dn(te, n_valid, act, w_dn, b_dn, tm, cfg["moe_tn"])
    return _combine(pos, x, wgt, g_next, y, cfg["combine_rows"])


def _rope_tables(pos):
    freqs = ROPE_THETA ** (-jnp.arange(0, ROT_DIM, 2, dtype=F32) / ROT_DIM)
    ang = pos.astype(F32)[:, None] * freqs[None, :]
    cos, sin = jnp.cos(ang), jnp.sin(ang)
    t = pos.shape[0]
    rest = HEAD_DIM - ROT_DIM
    cos_tab = jnp.concatenate([cos, cos, jnp.ones((t, rest), F32)], axis=1)
    sin_tab = jnp.concatenate([-sin, sin, jnp.zeros((t, rest), F32)], axis=1)
    return cos_tab, sin_tab


def _mix_matrix():
    src = jnp.arange(HEAD_DIM)[:, None]
    dst = jnp.arange(HEAD_DIM)[None, :]
    perm = ((dst < ROT_HALF) & (src == dst + ROT_HALF)) | (
        (dst >= ROT_HALF) & (dst < ROT_DIM) & (src == dst - ROT_HALF))
    zero = jnp.zeros((HEAD_DIM, HEAD_DIM), F32)
    top = jnp.concatenate([jnp.ones((HEAD_DIM, HEAD_DIM), F32), zero], axis=1)
    bot = jnp.concatenate([zero, perm.astype(F32)], axis=1)
    return jnp.concatenate([top, bot], axis=0).astype(BF16)


def _forward(x_prompt, x_sample, cache_k, cache_v, norm1_g, norm2_g, w_in, q_norm_g, k_norm_g,
             lambda_qk, subln_g, gmlp_ln_g, gmlp_ln_b, gmlp_w_s, gmlp_b_s, gate_b, w_proj_a,
             w_proj_b, w_out, w_router, b_router, w_gu, b_gu, w_dn, b_dn):
    batch, seq, d = x_prompt.shape
    dec_batch, dec_seq, _ = x_sample.shape
    depth, _, past = cache_k.shape[:3]
    gw = gmlp_ln_g.shape[1]
    qk = (w_in.shape[2] - 2 * gw - 2 * d) // 3
    n_heads = qk // V_DIM
    tp, ts = batch * seq, dec_batch * dec_seq
    t_all = tp + ts
    assert tp % ts == 0 and seq % GMLP_CHUNK == 0 and past % GMLP_CHUNK == 0

    cfg = dict(
        tm=_pick(t_all, (1536, 512, 256, 128)),
        tm_p=_pick(tp, (1024, 512, 256, 128)),
        tm_s=_pick(ts, (512, 256, 128, 64)),
        tn=_pick(qk, (512, 256)),
        tn_d=_pick(d, (512, 256)),
        tq=_pick(seq, (1024, 512, 256, 128)),
        attn_pair=2,
        router_tm=_pick(t_all, (256, 128)),
        moe_tm=256 if t_all >= 4096 else 64,
        moe_tn=_pick(d, (1024, 512, 256)),
        combine_rows=64,
        gmlp_sub=_pick(seq // GMLP_CHUNK, (4, 2, 1)),
    )
    tm, tn = cfg["tm"], cfg["tn"]

    pos_all = jnp.concatenate([jnp.tile(jnp.arange(seq), batch),
                               jnp.tile(past + jnp.arange(dec_seq), dec_batch)])
    rope = _rope_tables(pos_all)
    x = jnp.concatenate([x_prompt.reshape(tp, d), x_sample.reshape(ts, d)], axis=0)
    cache_k2 = cache_k.reshape(depth * dec_batch * past, qk)
    cache_v2 = cache_v.reshape(depth * dec_batch * past, qk)

    h = _prenorm(x, norm1_g[0], _pick(t_all, (512, 256, 128)))
    tiles = qk // LANES
    kout = jnp.zeros((depth * tp * tiles, LANES), F32)
    vout = jnp.zeros((depth * tp * tiles, LANES), F32)
    n_exp = w_router.shape[2]
    w_in_bf = w_in.astype(BF16).reshape(depth * d, -1)
    w_pa_bf = w_proj_a.astype(BF16).reshape(-1, d)
    w_pb_bf = w_proj_b.astype(BF16).reshape(-1, d)
    w_out_bf = w_out.astype(BF16).reshape(depth * d, d)
    w_gu3 = w_gu.reshape(depth * n_exp, d, -1)
    b_gu3 = b_gu.reshape(depth * n_exp, 1, -1)
    w_dn3 = w_dn.reshape(depth * n_exp, -1, d)
    b_dn3 = b_dn.reshape(depth * n_exp, 1, d)
    ks, vs, gs = [], [], []
    for l in range(depth):
        lam_init = 0.8 - 0.6 * math.exp(-0.3 * l)
        g_eff = (subln_g[l] * (1.0 - lam_init)).reshape(1, V_DIM)
        q_bf, kt_bf, v_bf, k_s, v_s, u_act, vg_n, gates, kout, vout = _project_in(
            h, w_in_bf, q_norm_g[l], k_norm_g[l], rope,
            gmlp_ln_g[l], gmlp_ln_b[l], gate_b[l], (qk, gw), cfg, l, tp, kout, vout)

        score_bound = (HEAD_DIM * Q_SCALE * 1.01 * jnp.max(jnp.abs(q_norm_g[l]))
                       * jnp.max(jnp.abs(k_norm_g[l])))
        attend = functools.partial(_attend_prompt, q_bf, kt_bf, v_bf, lambda_qk[l], g_eff,
                                   lam_init, batch, seq, cfg["tq"], cfg["attn_pair"])
        o_p = lax.cond(score_bound <= SCORE_BOUND_LIMIT,
                       lambda: attend(False), lambda: attend(True))
        o_s = _attend_sample(q_bf, k_s, v_s, cache_k2, cache_v2, l, lambda_qk[l], g_eff,
                             lam_init, tp, dec_batch, dec_seq, past)
        att = jnp.concatenate([o_p, o_s], axis=0)

        gm_p = _gmlp_mix("gmlp_prompt", u_act, vg_n, gmlp_w_s[l], gmlp_b_s[l].T, 0, tp,
                         GMLP_CHUNK, cfg["gmlp_sub"])
        gm_s = _gmlp_mix("gmlp_sample", u_act, vg_n, gmlp_w_s[l][:, :dec_seq, :dec_seq],
                         gmlp_b_s[l][:, :dec_seq].T, tp, ts, dec_seq, dec_batch)
        gm = jnp.concatenate([gm_p, gm_s], axis=0)

        merged = _merge(att, gm, w_pa_bf, w_pb_bf, l, gates, tm, cfg["tn_d"])
        x = _resid_matmul(x, merged, w_out_bf, l, tm, cfg["tn_d"])

        g_next = norm1_g[min(l + 1, depth - 1)]
        x, h = _moe(x, norm2_g[l], g_next, w_router[l], b_router[l], w_gu3, b_gu3, w_dn3,
                    b_dn3, l, cfg)

        ks.append(k_s.reshape(dec_batch, dec_seq, n_heads, 2, HEAD_DIM))
        vs.append(v_s.reshape(dec_batch, dec_seq, n_heads, V_DIM))
        gs.append(vg_n[tp:].reshape(dec_batch, dec_seq, gw))

    per_head = V_DIM // LANES
    new_k_prompt = kout.reshape(depth, batch, seq, n_heads, 2, HEAD_DIM)
    new_v_prompt = (vout.reshape(depth, batch, seq, per_head, n_heads, LANES)
                    .transpose(0, 1, 2, 4, 3, 5).reshape(depth, batch, seq, n_heads, V_DIM))
    return (x[:tp].reshape(batch, seq, d), x[tp:].reshape(dec_batch, dec_seq, d),
            new_k_prompt, new_v_prompt, jnp.stack(ks), jnp.stack(vs), jnp.stack(gs))


def kernel(x_prompt, x_sample, cache_k, cache_v, norm1_g, norm2_g, w_in, q_norm_g, k_norm_g, lambda_qk, subln_g, gmlp_ln_g, gmlp_ln_b, gmlp_w_s, gmlp_b_s, gate_b, w_proj_a, w_proj_b, w_out, w_router, b_router, w_gu, b_gu, w_dn, b_dn):
    return _forward(x_prompt, x_sample, cache_k, cache_v, norm1_g, norm2_g, w_in, q_norm_g,
                    k_norm_g, lambda_qk, subln_g, gmlp_ln_g, gmlp_ln_b, gmlp_w_s, gmlp_b_s,
                    gate_b, w_proj_a, w_proj_b, w_out, w_router, b_router, w_gu, b_gu, w_dn,
                    b_dn)
```

```python
import functools
import math

import jax
import jax.numpy as jnp
from jax import lax
from jax.experimental import pallas as pl
from jax.experimental.pallas import tpu as pltpu

F32 = jnp.float32
BF16 = jnp.bfloat16
U32 = jnp.uint32

LANES = 128
HEAD_DIM = 128
V_DIM = 2 * HEAD_DIM
CHUNK = 64
GMLP_CHUNK = 128
ROT_DIM = HEAD_DIM // 4
ROT_HALF = ROT_DIM // 2
ROPE_THETA = 500000.0
TOP_K = 4
SWIGLU_LIMIT = 7.0
SWIGLU_ALPHA = 1.702
EPS = 1e-6
NEG = -1e30
ATT_SCALE = 1.0 / math.sqrt(HEAD_DIM)
LOG2E = math.log2(math.e)
Q_SCALE = ATT_SCALE * LOG2E
GELU_C = math.sqrt(2.0 / math.pi)
SCORE_BOUND_LIMIT = 64.0
HI_MASK = 0xFFFF0000

VMEM_LIMIT = 52 * 1024 * 1024


def _params(*sem):
    return pltpu.CompilerParams(dimension_semantics=sem, vmem_limit_bytes=VMEM_LIMIT)


def _gelu(x):
    return 0.5 * x * (1.0 + jnp.tanh(GELU_C * (x + 0.044715 * (x * x * x))))


def _rms(x, g):
    return x * lax.rsqrt(jnp.mean(x * x, axis=-1, keepdims=True) + EPS) * g


def _pick(n, prefs):
    for p in prefs:
        if n % p == 0:
            return p
    return n


def _prenorm_kernel(x_ref, g_ref, o_ref):
    o_ref[...] = _rms(x_ref[...], g_ref[...]).astype(BF16)


def _prenorm(x, g, tm):
    t, d = x.shape
    return pl.pallas_call(
        _prenorm_kernel,
        grid=(t // tm,),
        in_specs=[pl.BlockSpec((tm, d), lambda i: (i, 0)),
                  pl.BlockSpec((1, d), lambda i: (0, 0))],
        out_specs=pl.BlockSpec((tm, d), lambda i: (i, 0)),
        out_shape=jax.ShapeDtypeStruct((t, d), BF16),
        compiler_params=_params("parallel"),
        name="prenorm",
    )(x, g.reshape(1, d))


def _mm_kernel(*refs, epilogue, n_extra, n_alias):
    h_ref, w_ref = refs[:2]
    extra = refs[2:2 + n_extra]
    outs = refs[2 + n_extra + n_alias:]
    z = jnp.dot(h_ref[...], w_ref[...], preferred_element_type=F32)
    epilogue(z, pl.program_id(1), extra, outs)


def _matmul_epi(name, h, w, col0, ncols, tm, tn, epilogue, extras, extra_specs,
                out_shapes, out_specs, row0=0, rows=None, aliased=(), wrow=0):
    t, d = h.shape
    rows = t - row0 if rows is None else rows
    assert rows % tm == 0 and row0 % tm == 0 and ncols % tn == 0 and col0 % tn == 0
    rb0, jb = row0 // tm, col0 // tn
    in_specs = [
        pl.BlockSpec((tm, d), lambda i, j: (rb0 + i, 0)),
        pl.BlockSpec((d, tn), lambda i, j: (wrow, jb + j)),
    ] + list(extra_specs) + [pl.BlockSpec(memory_space=pl.ANY)] * len(aliased)
    n_in = 2 + len(extras)
    kern = functools.partial(_mm_kernel, epilogue=epilogue, n_extra=len(extras),
                             n_alias=len(aliased))
    return pl.pallas_call(
        kern,
        grid=(rows // tm, ncols // tn),
        in_specs=in_specs,
        out_specs=out_specs,
        out_shape=out_shapes,
        input_output_aliases={n_in + a: a for a in range(len(aliased))},
        compiler_params=_params("parallel", "arbitrary"),
        name=name,
    )(h, w, *extras, *aliased)


def _headnorm_rope(zc, mix, ct, st):
    lhs = jnp.concatenate([(zc * zc).astype(BF16), zc.astype(BF16)], axis=1)
    red = jnp.dot(lhs, mix, preferred_element_type=F32)
    rstd = lax.rsqrt(red[:, :HEAD_DIM] * (1.0 / HEAD_DIM) + EPS)
    return (zc * ct + red[:, HEAD_DIM:] * st) * rstd


def _head_slices(tn):
    return [slice(h * HEAD_DIM, (h + 1) * HEAD_DIM) for h in range(tn // HEAD_DIM)]


def _store_token_major(out_ref, j, n_tiles, pieces, tiles, dest):
    tm = pieces[0].shape[0]
    for jj in range(n_tiles):
        @pl.when(j == jj)
        def _(jj=jj):
            for hh, r in enumerate(pieces):
                out_ref[pl.ds(dest(jj * len(pieces) + hh), tm, stride=tiles), :] = r


def _epi_q(z, j, extra, outs, *, tn):
    mix, ct, st = (r[...] for r in extra)
    for sl in _head_slices(tn):
        outs[0][:, sl] = _headnorm_rope(z[:, sl], mix, ct, st).astype(BF16)


def _epi_k_prompt(z, j, extra, outs, *, tn, n_tiles):
    mix, ct, st = (r[...] for r in extra)
    kout_ref, kt_ref = outs
    pieces = []
    for sl in _head_slices(tn):
        r = _headnorm_rope(z[:, sl], mix, ct, st)
        kt_ref[sl, :] = r.T.astype(BF16)
        pieces.append(r)
    _store_token_major(kout_ref, j, n_tiles, pieces, n_tiles * len(pieces), lambda c: c)


def _epi_k_sample(z, j, extra, outs, *, tn):
    mix, ct, st = (r[...] for r in extra)
    for sl in _head_slices(tn):
        outs[0][:, sl] = _headnorm_rope(z[:, sl], mix, ct, st)


def _epi_v_prompt(z, j, extra, outs, *, tn, n_tiles):
    vout_ref, vbf_ref = outs
    vbf_ref[...] = z.astype(BF16)
    pieces = [z[:, sl] for sl in _head_slices(tn)]
    tiles = n_tiles * len(pieces)
    per_head = V_DIM // LANES
    _store_token_major(vout_ref, j, n_tiles, pieces, tiles,
                       lambda c: (c % per_head) * (tiles // per_head) + c // per_head)


def _epi_copy(z, j, extra, outs):
    for o in outs:
        o[...] = z.astype(o.dtype)


def _epi_gelu(z, j, extra, outs):
    outs[0][...] = _gelu(z).astype(outs[0].dtype)


def _epi_gelu_layernorm(z, j, extra, outs, *, tn, n_tiles):
    o = outs[0]
    a = _gelu(z)
    for jj in range(n_tiles):
        @pl.when(j == jj)
        def _(jj=jj):
            o[:, jj * tn:(jj + 1) * tn] = a

    @pl.when(j == n_tiles - 1)
    def _():
        full = o[...]
        mu = jnp.mean(full, axis=-1, keepdims=True)
        cen = full - mu
        var = jnp.mean(cen * cen, axis=-1, keepdims=True)
        o[...] = cen * lax.rsqrt(var + EPS) * extra[0][...] + extra[1][...]


def _epi_sigmoid_bias(z, j, extra, outs):
    outs[0][...] = jax.nn.sigmoid(z + extra[0][...]).astype(outs[0].dtype)


def _project_in(h, w_in, qg, kg, rope, ln_g, ln_b, gate_b, dims, cfg, layer, tp, kout, vout):
    t, d = h.shape
    ts = t - tp
    qk, gw = dims
    tm, tn, tm_p, tm_s = cfg["tm"], cfg["tn"], cfg["tm_p"], cfg["tm_s"]
    cos_tab, sin_tab = rope
    o1, o2, o3, o4, o5 = qk, 2 * qk, 3 * qk, 3 * qk + gw, 3 * qk + 2 * gw
    tiles = qk // LANES
    n_tiles = qk // tn
    vec = lambda n: pl.BlockSpec((1, n), lambda i, j: (0, 0))
    out_tile = lambda tmx: pl.BlockSpec((tmx, tn), lambda i, j: (i, j))
    sds = jax.ShapeDtypeStruct
    mix = _mix_matrix()

    def rope_specs(tmx, row0):
        rb0 = row0 // tmx
        row_tile = pl.BlockSpec((tmx, HEAD_DIM), lambda i, j: (rb0 + i, 0))
        return [pl.BlockSpec((2 * HEAD_DIM, 2 * HEAD_DIM), lambda i, j: (0, 0)),
                row_tile, row_tile]

    def gain_tables(gain, scale):
        lane = jnp.arange(HEAD_DIM)
        partner = jnp.where(lane < ROT_HALF, jnp.roll(gain, -ROT_HALF), jnp.roll(gain, ROT_HALF))
        return cos_tab * (gain * scale)[None, :], sin_tab * (partner * scale)[None, :]

    stacked = lambda: pl.BlockSpec((tm_p * tiles, LANES),
                                   lambda i, j: (layer * (tp // tm_p) + i, 0))
    k_tabs = gain_tables(kg, 1.0)

    (q_bf,) = _matmul_epi(
        "proj_q", h, w_in, 0, qk, tm, tn, functools.partial(_epi_q, tn=tn),
        [mix, *gain_tables(qg, Q_SCALE)], rope_specs(tm, 0),
        [sds((t, qk), BF16)], [out_tile(tm)], wrow=layer)
    kout, kt_bf = _matmul_epi(
        "proj_k_prompt", h, w_in, o1, qk, tm_p, tn,
        functools.partial(_epi_k_prompt, tn=tn, n_tiles=n_tiles),
        [mix, *k_tabs], rope_specs(tm_p, 0),
        [sds(kout.shape, F32), sds((qk, tp), BF16)],
        [stacked(), pl.BlockSpec((tn, tm_p), lambda i, j: (j, i))],
        rows=tp, aliased=(kout,), wrow=layer)
    (k_s,) = _matmul_epi(
        "proj_k_sample", h, w_in, o1, qk, tm_s, tn, functools.partial(_epi_k_sample, tn=tn),
        [mix, *k_tabs], rope_specs(tm_s, tp),
        [sds((ts, qk), F32)], [out_tile(tm_s)], row0=tp, wrow=layer)
    vout, v_bf = _matmul_epi(
        "proj_v_prompt", h, w_in, o2, qk, tm_p, tn,
        functools.partial(_epi_v_prompt, tn=tn, n_tiles=n_tiles), [], [],
        [sds(vout.shape, F32), sds((tp, qk), BF16)], [stacked(), out_tile(tm_p)],
        rows=tp, aliased=(vout,), wrow=layer)
    (v_s,) = _matmul_epi(
        "proj_v_sample", h, w_in, o2, qk, tm_s, tn, _epi_copy, [], [],
        [sds((ts, qk), F32)], [out_tile(tm_s)], row0=tp, wrow=layer)
    (u_act,) = _matmul_epi(
        "proj_u", h, w_in, o3, gw, tm, tn, _epi_gelu, [], [],
        [sds((t, gw), BF16)], [out_tile(tm)], wrow=layer)
    (vg_n,) = _matmul_epi(
        "proj_vg", h, w_in, o4, gw, tm, tn,
        functools.partial(_epi_gelu_layernorm, tn=tn, n_tiles=gw // tn),
        [ln_g.reshape(1, gw), ln_b.reshape(1, gw)], [vec(gw), vec(gw)],
        [sds((t, gw), F32)], [pl.BlockSpec((tm, gw), lambda i, j: (i, 0))], wrow=layer)
    (gates,) = _matmul_epi(
        "proj_gates", h, w_in, o5, 2 * d, tm, tn, _epi_sigmoid_bias,
        [gate_b.reshape(1, 2 * d)], [pl.BlockSpec((1, tn), lambda i, j: (0, j))],
        [sds((t, 2 * d), BF16)], [out_tile(tm)], wrow=layer)
    return q_bf, kt_bf, v_bf, k_s, v_s, u_act, vg_n, gates, kout, vout


def _lambda(lq_ref, lam_init):
    lq = lq_ref[...]
    a = jnp.sum(lq[0:1] * lq[1:2], axis=-1, keepdims=True)
    b = jnp.sum(lq[2:3] * lq[3:4], axis=-1, keepdims=True)
    return jnp.exp(a) - jnp.exp(b) + lam_init


def _attn_finish(lam, g, a1, l1, a2, l2):
    o = a1 / l1 - lam * (a2 / l2)
    return _rms(o, g).astype(BF16)


def _attn_prompt_kernel(lq_ref, g_ref, q_ref, kt_ref, v_ref, o_ref, *scratch,
                        tq, lam_init, pair, running_max):
    qi = pl.program_id(2)
    half = tq // 2
    state = (scratch[0::2], scratch[1::2])

    def block(off, width, r0, diag_col0):
        rows = tq - r0
        v = v_ref[pl.ds(off, width), :]
        for mp in range(2):
            a_ref, l_ref = state[mp][:2]
            q = q_ref[r0:, mp * HEAD_DIM:(mp + 1) * HEAD_DIM]
            kt = kt_ref[mp * HEAD_DIM:(mp + 1) * HEAD_DIM, pl.ds(off, width)]
            s = jnp.dot(q, kt, preferred_element_type=F32)
            vis = None
            if diag_col0 is not None:
                rc = (lax.broadcasted_iota(jnp.int32, (rows, width), 0) + r0) // CHUNK
                cc = (lax.broadcasted_iota(jnp.int32, (rows, width), 1) + diag_col0) // CHUNK
                vis = cc <= rc
            if running_max:
                m_ref = state[mp][2]
                if vis is not None:
                    s = jnp.where(vis, s, NEG)
                m_old = m_ref[r0:, :]
                m_new = jnp.maximum(m_old, jnp.max(s, axis=-1, keepdims=True))
                alpha = jnp.exp2(m_old - m_new)
                p = jnp.exp2(s - m_new)
                m_ref[r0:, :] = m_new
                l_ref[r0:, :] = alpha * l_ref[r0:, :] + jnp.sum(p, axis=-1, keepdims=True)
                a_ref[r0:, :] = alpha * a_ref[r0:, :] + jnp.dot(
                    p.astype(BF16), v, preferred_element_type=F32)
            else:
                p = jnp.exp2(s)
                if vis is not None:
                    p = jnp.where(vis, p, 0.0)
                l_ref[r0:, :] += jnp.sum(p, axis=-1, keepdims=True)
                a_ref[r0:, :] += jnp.dot(p.astype(BF16), v, preferred_element_type=F32)

    for refs in state:
        refs[0][...] = jnp.zeros(refs[0].shape, F32)
        refs[1][...] = jnp.zeros(refs[1].shape, F32)
        if running_max:
            refs[2][...] = jnp.full(refs[2].shape, NEG, F32)

    def body(j, _):
        off = pl.multiple_of(j * (pair * tq), pair * tq)
        for u in range(pair):
            block(off + u * tq, tq, 0, None)
        return 0

    lax.fori_loop(0, qi // pair, body, 0)
    for u in range(pair - 1):
        @pl.when(qi % pair > u)
        def _(u=u):
            block(pl.multiple_of((qi - 1 - u) * tq, tq), tq, 0, None)

    base = pl.multiple_of(qi * tq, tq)
    block(base, half, 0, 0)
    block(base + half, half, half, half)
    (a1, l1), (a2, l2) = (r[:2] for r in state)
    o_ref[...] = _attn_finish(_lambda(lq_ref, lam_init), g_ref[...], a1[...], l1[...],
                              a2[...], l2[...])


def _attend_prompt(q_bf, kt_bf, v_bf, lq, g_eff, lam_init, batch, seq, tq, pair, running_max):
    qk = q_bf.shape[1]
    n_heads = qk // V_DIM
    nq = seq // tq
    kern = functools.partial(_attn_prompt_kernel, tq=tq, lam_init=lam_init, pair=pair,
                             running_max=running_max)
    per_map = [pltpu.VMEM((tq, V_DIM), F32), pltpu.VMEM((tq, 1), F32)]
    if running_max:
        per_map.append(pltpu.VMEM((tq, 1), F32))
    scratch = [s for s in per_map for _ in range(2)]
    return pl.pallas_call(
        kern,
        grid=(batch, n_heads, nq),
        in_specs=[
            pl.BlockSpec((4, HEAD_DIM), lambda b, h, i: (0, 0)),
            pl.BlockSpec((1, V_DIM), lambda b, h, i: (0, 0)),
            pl.BlockSpec((tq, V_DIM), lambda b, h, i: (b * nq + i, h)),
            pl.BlockSpec((V_DIM, seq), lambda b, h, i: (h, b)),
            pl.BlockSpec((seq, V_DIM), lambda b, h, i: (b, h)),
        ],
        out_specs=pl.BlockSpec((tq, V_DIM), lambda b, h, i: (b * nq + i, h)),
        out_shape=jax.ShapeDtypeStruct((batch * seq, qk), BF16),
        scratch_shapes=scratch,
        compiler_params=_params("parallel", "parallel", "arbitrary"),
        name="attn_prompt_max" if running_max else "attn_prompt",
    )(lq, g_eff, q_bf, kt_bf, v_bf)


def _attn_sample_kernel(lq_ref, g_ref, q_ref, kc_ref, vc_ref, kn_ref, vn_ref, o_ref, *,
                        lam_init):
    q = q_ref[...]
    kc = kc_ref[...].astype(BF16)
    vc = vc_ref[...].astype(BF16)
    kn = kn_ref[...].astype(BF16)
    vn = vn_ref[...].astype(BF16)
    res = []
    for mp in range(2):
        sl = slice(mp * HEAD_DIM, (mp + 1) * HEAD_DIM)
        sc = lax.dot_general(q[:, sl], kc[:, sl], (((1,), (1,)), ((), ())),
                             preferred_element_type=F32)
        sn = lax.dot_general(q[:, sl], kn[:, sl], (((1,), (1,)), ((), ())),
                             preferred_element_type=F32)
        m = jnp.maximum(jnp.max(sc, axis=-1, keepdims=True),
                        jnp.max(sn, axis=-1, keepdims=True))
        pc = jnp.exp2(sc - m)
        pn = jnp.exp2(sn - m)
        l = jnp.sum(pc, axis=-1, keepdims=True) + jnp.sum(pn, axis=-1, keepdims=True)
        acc = (jnp.dot(pc.astype(BF16), vc, preferred_element_type=F32)
               + jnp.dot(pn.astype(BF16), vn, preferred_element_type=F32))
        res += [acc, l]
    o_ref[...] = _attn_finish(_lambda(lq_ref, lam_init), g_ref[...], *res)


def _attend_sample(q_bf, k_s, v_s, cache_k2, cache_v2, layer, lq, g_eff, lam_init,
                   row0, dec_batch, dec_seq, past):
    qk = q_bf.shape[1]
    n_heads = qk // V_DIM
    rb0 = row0 // dec_seq
    q_rows = pl.BlockSpec((dec_seq, V_DIM), lambda b, h: (rb0 + b, h))
    new_rows = pl.BlockSpec((dec_seq, V_DIM), lambda b, h: (b, h))
    cache_rows = pl.BlockSpec((past, V_DIM), lambda b, h: (layer * dec_batch + b, h))
    kern = functools.partial(_attn_sample_kernel, lam_init=lam_init)
    return pl.pallas_call(
        kern,
        grid=(dec_batch, n_heads),
        in_specs=[
            pl.BlockSpec((4, HEAD_DIM), lambda b, h: (0, 0)),
            pl.BlockSpec((1, V_DIM), lambda b, h: (0, 0)),
            q_rows, cache_rows, cache_rows, new_rows, new_rows,
        ],
        out_specs=pl.BlockSpec((dec_seq, V_DIM), lambda b, h: (b, h)),
        out_shape=jax.ShapeDtypeStruct((dec_batch * dec_seq, qk), BF16),
        compiler_params=_params("parallel", "parallel"),
        name="attn_sample",
    )(lq, g_eff, q_bf, cache_k2, cache_v2, k_s, v_s)


def _gmlp_kernel(w_ref, b_ref, u_ref, v_ref, o_ref, *, c, n_sub, groups):
    gd = u_ref.shape[1] // groups
    tri = (lax.broadcasted_iota(jnp.int32, (c, c), 0)
           >= lax.broadcasted_iota(jnp.int32, (c, c), 1))
    for g in range(groups):
        wg = jnp.where(tri, w_ref[g], 0.0).astype(BF16)
        bg = b_ref[:, g:g + 1]
        cs = slice(g * gd, (g + 1) * gd)
        for s in range(n_sub):
            rs = slice(s * c, (s + 1) * c)
            mixed = jnp.dot(wg, v_ref[rs, cs].astype(BF16), preferred_element_type=F32) + bg
            o_ref[rs, cs] = (u_ref[rs, cs].astype(F32) * mixed).astype(BF16)


def _gmlp_mix(name, u, vg, w_s, b_s_t, row0, rows, c, n_sub):
    groups = w_s.shape[0]
    gw = u.shape[1]
    tr = c * n_sub
    rb0 = row0 // tr
    rows_spec = pl.BlockSpec((tr, gw), lambda i: (rb0 + i, 0))
    kern = functools.partial(_gmlp_kernel, c=c, n_sub=n_sub, groups=groups)
    return pl.pallas_call(
        kern,
        grid=(rows // tr,),
        in_specs=[
            pl.BlockSpec((groups, c, c), lambda i: (0, 0, 0)),
            pl.BlockSpec((c, groups), lambda i: (0, 0)),
            rows_spec, rows_spec,
        ],
        out_specs=pl.BlockSpec((tr, gw), lambda i: (i, 0)),
        out_shape=jax.ShapeDtypeStruct((rows, gw), BF16),
        compiler_params=_params("parallel"),
        name=name,
    )(w_s, b_s_t, u, vg)


def _merge_kernel(a_ref, m_ref, wa_ref, wb_ref, g0_ref, g1_ref, o_ref):
    ya = jnp.dot(a_ref[...], wa_ref[...], preferred_element_type=F32)
    yb = jnp.dot(m_ref[...], wb_ref[...], preferred_element_type=F32)
    o_ref[...] = (g0_ref[...].astype(F32) * ya + g1_ref[...].astype(F32) * yb).astype(BF16)


def _merge(att, gm, w_pa, w_pb, layer, gates, tm, tn):
    t, wa = att.shape
    d = w_pa.shape[1]
    nj = d // tn
    return pl.pallas_call(
        _merge_kernel,
        grid=(t // tm, nj),
        in_specs=[
            pl.BlockSpec((tm, wa), lambda i, j: (i, 0)),
            pl.BlockSpec((tm, gm.shape[1]), lambda i, j: (i, 0)),
            pl.BlockSpec((wa, tn), lambda i, j: (layer, j)),
            pl.BlockSpec((gm.shape[1], tn), lambda i, j: (layer, j)),
            pl.BlockSpec((tm, tn), lambda i, j: (i, j)),
            pl.BlockSpec((tm, tn), lambda i, j: (i, nj + j)),
        ],
        out_specs=pl.BlockSpec((tm, tn), lambda i, j: (i, j)),
        out_shape=jax.ShapeDtypeStruct((t, d), BF16),
        compiler_params=_params("parallel", "arbitrary"),
        name="merge",
    )(att, gm, w_pa, w_pb, gates, gates)


def _resid_matmul_kernel(x_ref, m_ref, w_ref, o_ref):
    o_ref[...] = x_ref[...] + jnp.dot(m_ref[...], w_ref[...], preferred_element_type=F32)


def _resid_matmul(x, m, w, layer, tm, tn):
    t, d = x.shape
    return pl.pallas_call(
        _resid_matmul_kernel,
        grid=(t // tm, d // tn),
        in_specs=[
            pl.BlockSpec((tm, tn), lambda i, j: (i, j)),
            pl.BlockSpec((tm, m.shape[1]), lambda i, j: (i, 0)),
            pl.BlockSpec((m.shape[1], tn), lambda i, j: (layer, j)),
        ],
        out_specs=pl.BlockSpec((tm, tn), lambda i, j: (i, j)),
        out_shape=jax.ShapeDtypeStruct((t, d), F32),
        compiler_params=_params("parallel", "arbitrary"),
        name="out_proj",
    )(x, m, w)


def _router_kernel(x_ref, g_ref, wr_ref, br_ref, idx_ref, wgt_ref, hp_ref, cnt_ref, run_ref):
    tm, d = x_ref.shape

    @pl.when(pl.program_id(0) == 0)
    def _():
        run_ref[...] = jnp.zeros(run_ref.shape, F32)

    h = _rms(x_ref[...], g_ref[...])
    logits = jnp.dot(h, wr_ref[...], preferred_element_type=F32,
                     precision=lax.Precision.HIGHEST) + br_ref[...]
    n_exp = logits.shape[1]
    lane_e = lax.broadcasted_iota(jnp.int32, logits.shape, 1)
    lane_o = lax.broadcasted_iota(jnp.int32, idx_ref.shape, 1)
    idx_out = jnp.zeros(idx_ref.shape, jnp.int32)
    val_out = jnp.full(wgt_ref.shape, -jnp.inf, F32)
    top = None
    chosen = []
    for k in range(TOP_K):
        m = jnp.max(logits, axis=-1, keepdims=True)
        i = jnp.min(jnp.where(logits == m, lane_e, n_exp), axis=-1, keepdims=True)
        pick = lane_e == i
        chosen.append(pick)
        logits = jnp.where(pick, -jnp.inf, logits)
        idx_out = jnp.where(lane_o == k, i, idx_out)
        val_out = jnp.where(lane_o == k, m, val_out)
        if k == 0:
            top = m
    e = jnp.exp(val_out - top)
    wgt_ref[...] = e / jnp.sum(e, axis=-1, keepdims=True)

    multi = sum(jnp.where(p, 1.0, 0.0) for p in chosen)
    earlier = (lax.broadcasted_iota(jnp.int32, (tm, tm), 1)
               < lax.broadcasted_iota(jnp.int32, (tm, tm), 0))
    before = jnp.dot(jnp.where(earlier, 1.0, 0.0).astype(BF16), multi.astype(BF16),
                     preferred_element_type=F32) + run_ref[...]
    for k in range(TOP_K):
        rank = jnp.sum(jnp.where(chosen[k], before, 0.0), axis=-1, keepdims=True)
        idx_out = jnp.where(lane_o == TOP_K + k, rank.astype(jnp.int32), idx_out)
    idx_ref[...] = idx_out
    run_ref[...] += jnp.sum(multi, axis=0, keepdims=True)
    cnt_ref[...] = run_ref[...]

    nw = d // (2 * LANES)
    bits = lax.bitcast_convert_type(h.astype(BF16).astype(F32), U32)
    for c in range(nw):
        lo = bits[:, c * LANES:(c + 1) * LANES] >> 16
        hi = bits[:, (c + nw) * LANES:(c + nw + 1) * LANES] & jnp.uint32(HI_MASK)
        hp_ref[pl.ds(c, tm, stride=nw), :] = lo | hi


def _router(x, g, w_router, b_router, tm):
    t, d = x.shape
    n_exp = w_router.shape[1]
    nw = d // (2 * LANES)
    return pl.pallas_call(
        _router_kernel,
        grid=(t // tm,),
        in_specs=[
            pl.BlockSpec((tm, d), lambda i: (i, 0)),
            pl.BlockSpec((1, d), lambda i: (0, 0)),
            pl.BlockSpec((d, n_exp), lambda i: (0, 0)),
            pl.BlockSpec((1, n_exp), lambda i: (0, 0)),
        ],
        out_specs=[pl.BlockSpec((tm, LANES), lambda i: (i, 0)),
                   pl.BlockSpec((tm, LANES), lambda i: (i, 0)),
                   pl.BlockSpec((tm * nw, LANES), lambda i: (i, 0)),
                   pl.BlockSpec((1, n_exp), lambda i: (0, 0))],
        out_shape=[jax.ShapeDtypeStruct((t, LANES), jnp.int32),
                   jax.ShapeDtypeStruct((t, LANES), F32),
                   jax.ShapeDtypeStruct((t * nw, LANES), U32),
                   jax.ShapeDtypeStruct((1, n_exp), F32)],
        scratch_shapes=[pltpu.VMEM((1, n_exp), F32)],
        compiler_params=_params("arbitrary"),
        name="router",
    )(x, g.reshape(1, d), w_router, b_router.reshape(1, n_exp))


ISSUE_UNROLL = 8


def _dispatch_kernel(src_ref, hp_hbm, o_ref, buf, sem, *, rows, nw):
    i = pl.program_id(0)
    n = pl.num_programs(0)

    def row_copy(tok_row, slot, r):
        return pltpu.make_async_copy(hp_hbm.at[pl.ds(tok_row, nw), :],
                                     buf.at[slot, pl.ds(r * nw, nw), :], sem.at[slot])

    def issue(tile, slot):
        def body(rb, _):
            for u in range(ISSUE_UNROLL):
                r = rb * ISSUE_UNROLL + u
                tok_row = pl.multiple_of(src_ref[tile * rows + r] * nw, nw)
                row_copy(tok_row, slot, r).start(priority=u % 2)
            return 0
        lax.fori_loop(0, rows // ISSUE_UNROLL, body, 0)

    slot = i % 2

    @pl.when(i == 0)
    def _():
        issue(0, 0)

    @pl.when(i + 1 < n)
    def _():
        issue(i + 1, 1 - slot)

    for r in range(rows):
        row_copy(0, slot, r).wait()

    los, his = [], []
    for c in range(nw):
        w = buf[slot, pl.ds(c, rows, stride=nw), :]
        los.append(lax.bitcast_convert_type(w << 16, F32))
        his.append(lax.bitcast_convert_type(w & jnp.uint32(HI_MASK), F32))
    o_ref[...] = jnp.concatenate(los + his, axis=1).astype(BF16)


def _dispatch(src_tok, hp, d, n_tiles, rows):
    nw = d // (2 * LANES)
    kern = functools.partial(_dispatch_kernel, rows=rows, nw=nw)
    return pl.pallas_call(
        kern,
        grid_spec=pltpu.PrefetchScalarGridSpec(
            num_scalar_prefetch=1,
            grid=(n_tiles,),
            in_specs=[pl.BlockSpec(memory_space=pl.ANY)],
            out_specs=pl.BlockSpec((rows, d), lambda i, s: (i, 0)),
            scratch_shapes=[pltpu.VMEM((2, rows * nw, LANES), U32),
                            pltpu.SemaphoreType.DMA((2,))],
        ),
        out_shape=jax.ShapeDtypeStruct((n_tiles * rows, d), BF16),
        compiler_params=_params("arbitrary"),
        name="dispatch",
    )(src_tok, hp)


def _new_expert(te_ref, i):
    return (i == 0) | (te_ref[i] != te_ref[jnp.maximum(i - 1, 0)])


def _gmm_gu_kernel(te_ref, nt_ref, x_ref, wg_ref, wu_ref, bg_ref, bu_ref, o_ref, wg_bf, wu_bf):
    i = pl.program_id(1)

    @pl.when(_new_expert(te_ref, i))
    def _():
        wg_bf[...] = wg_ref[0].astype(BF16)
        wu_bf[...] = wu_ref[0].astype(BF16)

    @pl.when(i < nt_ref[0])
    def _():
        x = x_ref[...]
        gate = jnp.dot(x, wg_bf[...], preferred_element_type=F32) + bg_ref[0]
        up = jnp.dot(x, wu_bf[...], preferred_element_type=F32) + bu_ref[0]
        gate = jnp.minimum(gate, SWIGLU_LIMIT)
        up = jnp.clip(up, -SWIGLU_LIMIT, SWIGLU_LIMIT)
        act = (up + 1.0) * (gate * jax.nn.sigmoid(SWIGLU_ALPHA * gate))
        o_ref[...] = act.astype(BF16)

    @pl.when(i >= nt_ref[0])
    def _():
        o_ref[...] = jnp.zeros(o_ref.shape, BF16)


def _gmm_gu(tile_exp, n_valid, xs, w_gu, b_gu3, tm, tn):
    p, d = xs.shape
    f = w_gu.shape[2] // 2
    nj = f // tn
    return pl.pallas_call(
        _gmm_gu_kernel,
        grid_spec=pltpu.PrefetchScalarGridSpec(
            num_scalar_prefetch=2,
            grid=(nj, p // tm),
            in_specs=[
                pl.BlockSpec((tm, d), lambda j, i, te, nt: (i, 0)),
                pl.BlockSpec((1, d, tn), lambda j, i, te, nt: (te[i], 0, j)),
                pl.BlockSpec((1, d, tn), lambda j, i, te, nt: (te[i], 0, nj + j)),
                pl.BlockSpec((1, 1, tn), lambda j, i, te, nt: (te[i], 0, j)),
                pl.BlockSpec((1, 1, tn), lambda j, i, te, nt: (te[i], 0, nj + j)),
            ],
            out_specs=pl.BlockSpec((tm, tn), lambda j, i, te, nt: (i, j)),
            scratch_shapes=[pltpu.VMEM((d, tn), BF16), pltpu.VMEM((d, tn), BF16)],
        ),
        out_shape=jax.ShapeDtypeStruct((p, f), BF16),
        compiler_params=_params("arbitrary", "arbitrary"),
        name="expert_gate_up",
    )(tile_exp, n_valid, xs, w_gu, w_gu, b_gu3, b_gu3)


def _gmm_dn_kernel(te_ref, nt_ref, a_ref, w_ref, b_ref, o_ref, w_bf):
    i = pl.program_id(1)

    @pl.when(_new_expert(te_ref, i))
    def _():
        w_bf[...] = w_ref[0].astype(BF16)

    @pl.when(i < nt_ref[0])
    def _():
        o_ref[...] = jnp.dot(a_ref[...], w_bf[...], preferred_element_type=F32) + b_ref[0]

    @pl.when(i >= nt_ref[0])
    def _():
        o_ref[...] = jnp.zeros(o_ref.shape, F32)


def _gmm_dn(tile_exp, n_valid, act, w_dn, b_dn3, tm, tn):
    p, f = act.shape
    d = w_dn.shape[2]
    return pl.pallas_call(
        _gmm_dn_kernel,
        grid_spec=pltpu.PrefetchScalarGridSpec(
            num_scalar_prefetch=2,
            grid=(d // tn, p // tm),
            in_specs=[
                pl.BlockSpec((tm, f), lambda j, i, te, nt: (i, 0)),
                pl.BlockSpec((1, f, tn), lambda j, i, te, nt: (te[i], 0, j)),
                pl.BlockSpec((1, 1, tn), lambda j, i, te, nt: (te[i], 0, j)),
            ],
            out_specs=pl.BlockSpec((tm, tn), lambda j, i, te, nt: (i, j)),
            scratch_shapes=[pltpu.VMEM((f, tn), BF16)],
        ),
        out_shape=jax.ShapeDtypeStruct((p, d), F32),
        compiler_params=_params("arbitrary", "arbitrary"),
        name="expert_down",
    )(tile_exp, n_valid, act, w_dn, b_dn3)


def _combine_kernel(pos_ref, x_ref, w_ref, g_ref, y_hbm, o_ref, h_ref, buf, sem, *, rows):
    i = pl.program_id(0)
    n = pl.num_programs(0)

    def row_copy(src_row, slot, k, r):
        return pltpu.make_async_copy(y_hbm.at[pl.ds(src_row, 1), :],
                                     buf.at[slot, k, pl.ds(r, 1), :], sem.at[slot])

    def issue(tile, slot):
        def body(rb, _):
            for u in range(ISSUE_UNROLL // TOP_K):
                r = rb * (ISSUE_UNROLL // TOP_K) + u
                base = (tile * rows + r) * TOP_K
                for k in range(TOP_K):
                    row_copy(pos_ref[base + k], slot, k, r).start(priority=k % 2)
            return 0
        lax.fori_loop(0, rows * TOP_K // ISSUE_UNROLL, body, 0)

    slot = i % 2

    @pl.when(i == 0)
    def _():
        issue(0, 0)

    @pl.when(i + 1 < n)
    def _():
        issue(i + 1, 1 - slot)

    for r in range(rows):
        for k in range(TOP_K):
            row_copy(0, slot, k, r).wait()

    w = w_ref[...]
    acc = x_ref[...]
    for k in range(TOP_K):
        acc = acc + w[:, k:k + 1] * buf[slot, k]
    o_ref[...] = acc
    h_ref[...] = _rms(acc, g_ref[...]).astype(BF16)


def _combine(pos, x, wgt, g_next, y, rows):
    t, d = x.shape
    kern = functools.partial(_combine_kernel, rows=rows)
    return pl.pallas_call(
        kern,
        grid_spec=pltpu.PrefetchScalarGridSpec(
            num_scalar_prefetch=1,
            grid=(t // rows,),
            in_specs=[
                pl.BlockSpec((rows, d), lambda i, s: (i, 0)),
                pl.BlockSpec((rows, wgt.shape[1]), lambda i, s: (i, 0)),
                pl.BlockSpec((1, d), lambda i, s: (0, 0)),
                pl.BlockSpec(memory_space=pl.ANY),
            ],
            out_specs=[pl.BlockSpec((rows, d), lambda i, s: (i, 0)),
                       pl.BlockSpec((rows, d), lambda i, s: (i, 0))],
            scratch_shapes=[pltpu.VMEM((2, TOP_K, rows, d), F32),
                            pltpu.SemaphoreType.DMA((2,))],
        ),
        out_shape=[jax.ShapeDtypeStruct((t, d), F32), jax.ShapeDtypeStruct((t, d), BF16)],
        compiler_params=_params("arbitrary"),
        name="combine",
    )(pos, x, wgt, g_next.reshape(1, d), y)


def _route_plan(top_i, rank, counts, tm, n_tiles):
    t = top_i.shape[0]
    n_exp = counts.shape[0]
    a = t * TOP_K
    flat_e = top_i.reshape(a)
    rank = rank.reshape(a)
    padded = ((counts + tm - 1) // tm) * tm
    pend = jnp.cumsum(padded)
    pstart = pend - padded
    cstart = jnp.cumsum(counts) - counts
    n_valid = (pend[-1] // tm).astype(jnp.int32)
    pos = (pstart[flat_e] + rank).astype(jnp.int32)

    tile_ids = jnp.arange(n_tiles, dtype=jnp.int32)
    te = jnp.searchsorted(pend, tile_ids * tm, side="right").astype(jnp.int32)
    te = jnp.minimum(te, n_exp - 1)
    te = jnp.where(tile_ids < n_valid, te, te[jnp.maximum(n_valid - 1, 0)])

    order = jnp.argsort(flat_e, stable=True).astype(jnp.int32)
    p = jnp.arange(n_tiles * tm, dtype=jnp.int32)
    e_p = jnp.repeat(te, tm)
    r = p - pstart[e_p]
    valid = (r < counts[e_p]) & (p < pend[-1])
    src_sorted = jnp.clip(cstart[e_p] + r, 0, a - 1)
    src_tok = jnp.where(valid, order[src_sorted] // TOP_K, 0).astype(jnp.int32)
    return src_tok, pos, te, n_valid.reshape(1)


def _moe(x, g, g_next, w_router, b_router, w_gu, b_gu, w_dn, b_dn, layer, cfg):
    t, d = x.shape
    n_exp = w_router.shape[1]
    tm = cfg["moe_tm"]
    n_tiles = (t * TOP_K + n_exp * (tm - 1) + tm - 1) // tm
    idx, wgt, hp, cnt = _router(x, g, w_router, b_router, cfg["router_tm"])
    src_tok, pos, te, n_valid = _route_plan(idx[:, :TOP_K], idx[:, TOP_K:2 * TOP_K],
                                            cnt[0].astype(jnp.int32), tm, n_tiles)
    te = te + layer * n_exp
    xs = _dispatch(src_tok, hp, d, n_tiles, tm)
    act = _gmm_gu(te, n_valid, xs, w_gu, b_gu, tm, cfg["moe_tn"])
    y = _gmm_dn(te, n_valid, act, w_dn, b_dn, tm, cfg["moe_tn"])
    return _combine(pos, x, wgt, g_next, y, cfg["combine_rows"])


def _rope_tables(pos):
    freqs = ROPE_THETA ** (-jnp.arange(0, ROT_DIM, 2, dtype=F32) / ROT_DIM)
    ang = pos.astype(F32)[:, None] * freqs[None, :]
    cos, sin = jnp.cos(ang), jnp.sin(ang)
    t = pos.shape[0]
    rest = HEAD_DIM - ROT_DIM
    cos_tab = jnp.concatenate([cos, cos, jnp.ones((t, rest), F32)], axis=1)
    sin_tab = jnp.concatenate([-sin, sin, jnp.zeros((t, rest), F32)], axis=1)
    return cos_tab, sin_tab


def _mix_matrix():
    src = jnp.arange(HEAD_DIM)[:, None]
    dst = jnp.arange(HEAD_DIM)[None, :]
    perm = ((dst < ROT_HALF) & (src == dst + ROT_HALF)) | (
        (dst >= ROT_HALF) & (dst < ROT_DIM) & (src == dst - ROT_HALF))
    zero = jnp.zeros((HEAD_DIM, HEAD_DIM), F32)
    top = jnp.concatenate([jnp.ones((HEAD_DIM, HEAD_DIM), F32), zero], axis=1)
    bot = jnp.concatenate([zero, perm.astype(F32)], axis=1)
    return jnp.concatenate([top, bot], axis=0).astype(BF16)


def _forward(x_prompt, x_sample, cache_k, cache_v, norm1_g, norm2_g, w_in, q_norm_g, k_norm_g,
             lambda_qk, subln_g, gmlp_ln_g, gmlp_ln_b, gmlp_w_s, gmlp_b_s, gate_b, w_proj_a,
             w_proj_b, w_out, w_router, b_router, w_gu, b_gu, w_dn, b_dn):
    batch, seq, d = x_prompt.shape
    dec_batch, dec_seq, _ = x_sample.shape
    depth, _, past = cache_k.shape[:3]
    gw = gmlp_ln_g.shape[1]
    qk = (w_in.shape[2] - 2 * gw - 2 * d) // 3
    n_heads = qk // V_DIM
    tp, ts = batch * seq, dec_batch * dec_seq
    t_all = tp + ts
    assert tp % ts == 0 and seq % GMLP_CHUNK == 0 and past % GMLP_CHUNK == 0

    cfg = dict(
        tm=_pick(t_all, (1536, 512, 256, 128)),
        tm_p=_pick(tp, (1024, 512, 256, 128)),
        tm_s=_pick(ts, (512, 256, 128, 64)),
        tn=_pick(qk, (512, 256)),
        tn_d=_pick(d, (512, 256)),
        tq=_pick(seq, (1024, 512, 256, 128)),
        attn_pair=2,
        router_tm=_pick(t_all, (256, 128)),
        moe_tm=256 if t_all >= 4096 else 64,
        moe_tn=_pick(d, (1024, 512, 256)),
        combine_rows=64,
        gmlp_sub=_pick(seq // GMLP_CHUNK, (4, 2, 1)),
    )
    tm, tn = cfg["tm"], cfg["tn"]

    pos_all = jnp.concatenate([jnp.tile(jnp.arange(seq), batch),
                               jnp.tile(past + jnp.arange(dec_seq), dec_batch)])
    rope = _rope_tables(pos_all)
    x = jnp.concatenate([x_prompt.reshape(tp, d), x_sample.reshape(ts, d)], axis=0)
    cache_k2 = cache_k.reshape(depth * dec_batch * past, qk)
    cache_v2 = cache_v.reshape(depth * dec_batch * past, qk)

    h = _prenorm(x, norm1_g[0], _pick(t_all, (512, 256, 128)))
    tiles = qk // LANES
    kout = jnp.zeros((depth * tp * tiles, LANES), F32)
    vout = jnp.zeros((depth * tp * tiles, LANES), F32)
    n_exp = w_router.shape[2]
    w_in_bf = w_in.astype(BF16).reshape(depth * d, -1)
    w_pa_bf = w_proj_a.astype(BF16).reshape(-1, d)
    w_pb_bf = w_proj_b.astype(BF16).reshape(-1, d)
    w_out_bf = w_out.astype(BF16).reshape(depth * d, d)
    w_gu3 = w_gu.reshape(depth * n_exp, d, -1)
    b_gu3 = b_gu.reshape(depth * n_exp, 1, -1)
    w_dn3 = w_dn.reshape(depth * n_exp, -1, d)
    b_dn3 = b_dn.reshape(depth * n_exp, 1, d)
    ks, vs, gs = [], [], []
    for l in range(depth):
        lam_init = 0.8 - 0.6 * math.exp(-0.3 * l)
        g_eff = (subln_g[l] * (1.0 - lam_init)).reshape(1, V_DIM)
        q_bf, kt_bf, v_bf, k_s, v_s, u_act, vg_n, gates, kout, vout = _project_in(
            h, w_in_bf, q_norm_g[l], k_norm_g[l], rope,
            gmlp_ln_g[l], gmlp_ln_b[l], gate_b[l], (qk, gw), cfg, l, tp, kout, vout)

        score_bound = (HEAD_DIM * Q_SCALE * 1.01 * jnp.max(jnp.abs(q_norm_g[l]))
                       * jnp.max(jnp.abs(k_norm_g[l])))
        attend = functools.partial(_attend_prompt, q_bf, kt_bf, v_bf, lambda_qk[l], g_eff,
                                   lam_init, batch, seq, cfg["tq"], cfg["attn_pair"])
        o_p = lax.cond(score_bound <= SCORE_BOUND_LIMIT,
                       lambda: attend(False), lambda: attend(True))
        o_s = _attend_sample(q_bf, k_s, v_s, cache_k2, cache_v2, l, lambda_qk[l], g_eff,
                             lam_init, tp, dec_batch, dec_seq, past)
        att = jnp.concatenate([o_p, o_s], axis=0)

        gm_p = _gmlp_mix("gmlp_prompt", u_act, vg_n, gmlp_w_s[l], gmlp_b_s[l].T, 0, tp,
                         GMLP_CHUNK, cfg["gmlp_sub"])
        gm_s = _gmlp_mix("gmlp_sample", u_act, vg_n, gmlp_w_s[l][:, :dec_seq, :dec_seq],
                         gmlp_b_s[l][:, :dec_seq].T, tp, ts, dec_seq, dec_batch)
        gm = jnp.concatenate([gm_p, gm_s], axis=0)

        merged = _merge(att, gm, w_pa_bf, w_pb_bf, l, gates, tm, cfg["tn_d"])
        x = _resid_matmul(x, merged, w_out_bf, l, tm, cfg["tn_d"])

        g_next = norm1_g[min(l + 1, depth - 1)]
        x, h = _moe(x, norm2_g[l], g_next, w_router[l], b_router[l], w_gu3, b_gu3, w_dn3,
                    b_dn3, l, cfg)

        ks.append(k_s.reshape(dec_batch, dec_seq, n_heads, 2, HEAD_DIM))
        vs.append(v_s.reshape(dec_batch, dec_seq, n_heads, V_DIM))
        gs.append(vg_n[tp:].reshape(dec_batch, dec_seq, gw))

    per_head = V_DIM // LANES
    new_k_prompt = kout.reshape(depth, batch, seq, n_heads, 2, HEAD_DIM)
    new_v_prompt = (vout.reshape(depth, batch, seq, per_head, n_heads, LANES)
                    .transpose(0, 1, 2, 4, 3, 5).reshape(depth, batch, seq, n_heads, V_DIM))
    return (x[:tp].reshape(batch, seq, d), x[tp:].reshape(dec_batch, dec_seq, d),
            new_k_prompt, new_v_prompt, jnp.stack(ks), jnp.stack(vs), jnp.stack(gs))


def kernel(x_prompt, x_sample, cache_k, cache_v, norm1_g, norm2_g, w_in, q_norm_g, k_norm_g, lambda_qk, subln_g, gmlp_ln_g, gmlp_ln_b, gmlp_w_s, gmlp_b_s, gate_b, w_proj_a, w_proj_b, w_out, w_router, b_router, w_gu, b_gu, w_dn, b_dn):
    return _forward(x_prompt, x_sample, cache_k, cache_v, norm1_g, norm2_g, w_in, q_norm_g,
                    k_norm_g, lambda_qk, subln_g, gmlp_ln_g, gmlp_ln_b, gmlp_w_s, gmlp_b_s,
                    gate_b, w_proj_a, w_proj_b, w_out, w_router, b_router, w_gu, b_gu, w_dn,
                    b_dn)
```

```python
import functools
import math

import jax
import jax.numpy as jnp
from jax import lax
from jax.experimental import pallas as pl
from jax.experimental.pallas import tpu as pltpu

F32 = jnp.float32
BF16 = jnp.bfloat16
U32 = jnp.uint32

LANES = 128
HEAD_DIM = 128
V_DIM = 2 * HEAD_DIM
CHUNK = 64
GMLP_CHUNK = 128
ROT_DIM = HEAD_DIM // 4
ROT_HALF = ROT_DIM // 2
ROPE_THETA = 500000.0
TOP_K = 4
SWIGLU_LIMIT = 7.0
SWIGLU_ALPHA = 1.702
EPS = 1e-6
NEG = -1e30
ATT_SCALE = 1.0 / math.sqrt(HEAD_DIM)
LOG2E = math.log2(math.e)
Q_SCALE = ATT_SCALE * LOG2E
GELU_C = math.sqrt(2.0 / math.pi)
SCORE_BOUND_LIMIT = 64.0
HI_MASK = 0xFFFF0000

VMEM_LIMIT = 52 * 1024 * 1024


def _params(*sem):
    return pltpu.CompilerParams(dimension_semantics=sem, vmem_limit_bytes=VMEM_LIMIT)


def _gelu(x):
    return 0.5 * x * (1.0 + jnp.tanh(GELU_C * (x + 0.044715 * (x * x * x))))


def _rms(x, g):
    return x * lax.rsqrt(jnp.mean(x * x, axis=-1, keepdims=True) + EPS) * g


def _pick(n, prefs):
    for p in prefs:
        if n % p == 0:
            return p
    return n


def _prenorm_kernel(x_ref, g_ref, o_ref):
    o_ref[...] = _rms(x_ref[...], g_ref[...]).astype(BF16)


def _prenorm(x, g, tm):
    t, d = x.shape
    return pl.pallas_call(
        _prenorm_kernel,
        grid=(t // tm,),
        in_specs=[pl.BlockSpec((tm, d), lambda i: (i, 0)),
                  pl.BlockSpec((1, d), lambda i: (0, 0))],
        out_specs=pl.BlockSpec((tm, d), lambda i: (i, 0)),
        out_shape=jax.ShapeDtypeStruct((t, d), BF16),
        compiler_params=_params("parallel"),
        name="prenorm",
    )(x, g.reshape(1, d))


def _mm_kernel(*refs, epilogue, n_extra, n_alias):
    h_ref, w_ref = refs[:2]
    extra = refs[2:2 + n_extra]
    outs = refs[2 + n_extra + n_alias:]
    z = jnp.dot(h_ref[...], w_ref[...], preferred_element_type=F32)
    epilogue(z, pl.program_id(1), extra, outs)


def _matmul_epi(name, h, w, col0, ncols, tm, tn, epilogue, extras, extra_specs,
                out_shapes, out_specs, row0=0, rows=None, aliased=(), wrow=0):
    t, d = h.shape
    rows = t - row0 if rows is None else rows
    assert rows % tm == 0 and row0 % tm == 0 and ncols % tn == 0 and col0 % tn == 0
    rb0, jb = row0 // tm, col0 // tn
    in_specs = [
        pl.BlockSpec((tm, d), lambda i, j: (rb0 + i, 0)),
        pl.BlockSpec((d, tn), lambda i, j: (wrow, jb + j)),
    ] + list(extra_specs) + [pl.BlockSpec(memory_space=pl.ANY)] * len(aliased)
    n_in = 2 + len(extras)
    kern = functools.partial(_mm_kernel, epilogue=epilogue, n_extra=len(extras),
                             n_alias=len(aliased))
    return pl.pallas_call(
        kern,
        grid=(rows // tm, ncols // tn),
        in_specs=in_specs,
        out_specs=out_specs,
        out_shape=out_shapes,
        input_output_aliases={n_in + a: a for a in range(len(aliased))},
        compiler_params=_params("parallel", "arbitrary"),
        name=name,
    )(h, w, *extras, *aliased)


def _headnorm_rope(zc, mix, ct, st):
    lhs = jnp.concatenate([(zc * zc).astype(BF16), zc.astype(BF16)], axis=1)
    red = jnp.dot(lhs, mix, preferred_element_type=F32)
    rstd = lax.rsqrt(red[:, :HEAD_DIM] * (1.0 / HEAD_DIM) + EPS)
    return (zc * ct + red[:, HEAD_DIM:] * st) * rstd


def _head_slices(tn):
    return [slice(h * HEAD_DIM, (h + 1) * HEAD_DIM) for h in range(tn // HEAD_DIM)]


def _store_token_major(out_ref, j, n_tiles, pieces, tiles, dest):
    tm = pieces[0].shape[0]
    for jj in range(n_tiles):
        @pl.when(j == jj)
        def _(jj=jj):
            for hh, r in enumerate(pieces):
                out_ref[pl.ds(dest(jj * len(pieces) + hh), tm, stride=tiles), :] = r


def _epi_q(z, j, extra, outs, *, tn):
    mix, ct, st = (r[...] for r in extra)
    for sl in _head_slices(tn):
        outs[0][:, sl] = _headnorm_rope(z[:, sl], mix, ct, st).astype(BF16)


def _epi_k_prompt(z, j, extra, outs, *, tn, n_tiles):
    mix, ct, st = (r[...] for r in extra)
    kout_ref, kt_ref = outs
    pieces = []
    for sl in _head_slices(tn):
        r = _headnorm_rope(z[:, sl], mix, ct, st)
        kt_ref[sl, :] = r.T.astype(BF16)
        pieces.append(r)
    _store_token_major(kout_ref, j, n_tiles, pieces, n_tiles * len(pieces), lambda c: c)


def _epi_k_sample(z, j, extra, outs, *, tn):
    mix, ct, st = (r[...] for r in extra)
    for sl in _head_slices(tn):
        outs[0][:, sl] = _headnorm_rope(z[:, sl], mix, ct, st)


def _epi_v_prompt(z, j, extra, outs, *, tn, n_tiles):
    vout_ref, vbf_ref = outs
    vbf_ref[...] = z.astype(BF16)
    pieces = [z[:, sl] for sl in _head_slices(tn)]
    tiles = n_tiles * len(pieces)
    per_head = V_DIM // LANES
    _store_token_major(vout_ref, j, n_tiles, pieces, tiles,
                       lambda c: (c % per_head) * (tiles // per_head) + c // per_head)


def _epi_copy(z, j, extra, outs):
    for o in outs:
        o[...] = z.astype(o.dtype)


def _epi_gelu(z, j, extra, outs):
    outs[0][...] = _gelu(z).astype(outs[0].dtype)


def _epi_gelu_layernorm(z, j, extra, outs, *, tn, n_tiles):
    o = outs[0]
    a = _gelu(z)
    for jj in range(n_tiles):
        @pl.when(j == jj)
        def _(jj=jj):
            o[:, jj * tn:(jj + 1) * tn] = a

    @pl.when(j == n_tiles - 1)
    def _():
        full = o[...]
        mu = jnp.mean(full, axis=-1, keepdims=True)
        cen = full - mu
        var = jnp.mean(cen * cen, axis=-1, keepdims=True)
        o[...] = cen * lax.rsqrt(var + EPS) * extra[0][...] + extra[1][...]


def _epi_sigmoid_bias(z, j, extra, outs):
    outs[0][...] = jax.nn.sigmoid(z + extra[0][...]).astype(outs[0].dtype)


def _project_in(h, w_in, qg, kg, rope, ln_g, ln_b, gate_b, dims, cfg, layer, tp, kout, vout):
    t, d = h.shape
    ts = t - tp
    qk, gw = dims
    tm, tn, tm_p, tm_s = cfg["tm"], cfg["tn"], cfg["tm_p"], cfg["tm_s"]
    cos_tab, sin_tab = rope
    o1, o2, o3, o4, o5 = qk, 2 * qk, 3 * qk, 3 * qk + gw, 3 * qk + 2 * gw
    tiles = qk // LANES
    n_tiles = qk // tn
    vec = lambda n: pl.BlockSpec((1, n), lambda i, j: (0, 0))
    out_tile = lambda tmx: pl.BlockSpec((tmx, tn), lambda i, j: (i, j))
    sds = jax.ShapeDtypeStruct
    mix = _mix_matrix()

    def rope_specs(tmx, row0):
        rb0 = row0 // tmx
        row_tile = pl.BlockSpec((tmx, HEAD_DIM), lambda i, j: (rb0 + i, 0))
        return [pl.BlockSpec((2 * HEAD_DIM, 2 * HEAD_DIM), lambda i, j: (0, 0)),
                row_tile, row_tile]

    def gain_tables(gain, scale):
        lane = jnp.arange(HEAD_DIM)
        partner = jnp.where(lane < ROT_HALF, jnp.roll(gain, -ROT_HALF), jnp.roll(gain, ROT_HALF))
        return cos_tab * (gain * scale)[None, :], sin_tab * (partner * scale)[None, :]

    stacked = lambda: pl.BlockSpec((tm_p * tiles, LANES),
                                   lambda i, j: (layer * (tp // tm_p) + i, 0))
    k_tabs = gain_tables(kg, 1.0)

    (q_bf,) = _matmul_epi(
        "proj_q", h, w_in, 0, qk, tm, tn, functools.partial(_epi_q, tn=tn),
        [mix, *gain_tables(qg, Q_SCALE)], rope_specs(tm, 0),
        [sds((t, qk), BF16)], [out_tile(tm)], wrow=layer)
    kout, kt_bf = _matmul_epi(
        "proj_k_prompt", h, w_in, o1, qk, tm_p, tn,
        functools.partial(_epi_k_prompt, tn=tn, n_tiles=n_tiles),
        [mix, *k_tabs], rope_specs(tm_p, 0),
        [sds(kout.shape, F32), sds((qk, tp), BF16)],
        [stacked(), pl.BlockSpec((tn, tm_p), lambda i, j: (j, i))],
        rows=tp, aliased=(kout,), wrow=layer)
    (k_s,) = _matmul_epi(
        "proj_k_sample", h, w_in, o1, qk, tm_s, tn, functools.partial(_epi_k_sample, tn=tn),
        [mix, *k_tabs], rope_specs(tm_s, tp),
        [sds((ts, qk), F32)], [out_tile(tm_s)], row0=tp, wrow=layer)
    vout, v_bf = _matmul_epi(
        "proj_v_prompt", h, w_in, o2, qk, tm_p, tn,
        functools.partial(_epi_v_prompt, tn=tn, n_tiles=n_tiles), [], [],
        [sds(vout.shape, F32), sds((tp, qk), BF16)], [stacked(), out_tile(tm_p)],
        rows=tp, aliased=(vout,), wrow=layer)
    (v_s,) = _matmul_epi(
        "proj_v_sample", h, w_in, o2, qk, tm_s, tn, _epi_copy, [], [],
        [sds((ts, qk), F32)], [out_tile(tm_s)], row0=tp, wrow=layer)
    (u_act,) = _matmul_epi(
        "proj_u", h, w_in, o3, gw, tm, tn, _epi_gelu, [], [],
        [sds((t, gw), BF16)], [out_tile(tm)], wrow=layer)
    (vg_n,) = _matmul_epi(
        "proj_vg", h, w_in, o4, gw, tm, tn,
        functools.partial(_epi_gelu_layernorm, tn=tn, n_tiles=gw // tn),
        [ln_g.reshape(1, gw), ln_b.reshape(1, gw)], [vec(gw), vec(gw)],
        [sds((t, gw), F32)], [pl.BlockSpec((tm, gw), lambda i, j: (i, 0))], wrow=layer)
    (gates,) = _matmul_epi(
        "proj_gates", h, w_in, o5, 2 * d, tm, tn, _epi_sigmoid_bias,
        [gate_b.reshape(1, 2 * d)], [pl.BlockSpec((1, tn), lambda i, j: (0, j))],
        [sds((t, 2 * d), BF16)], [out_tile(tm)], wrow=layer)
    return q_bf, kt_bf, v_bf, k_s, v_s, u_act, vg_n, gates, kout, vout


def _lambda(lq_ref, lam_init):
    lq = lq_ref[...]
    a = jnp.sum(lq[0:1] * lq[1:2], axis=-1, keepdims=True)
    b = jnp.sum(lq[2:3] * lq[3:4], axis=-1, keepdims=True)
    return jnp.exp(a) - jnp.exp(b) + lam_init


def _attn_finish(lam, g, a1, l1, a2, l2):
    o = a1 / l1 - lam * (a2 / l2)
    return _rms(o, g).astype(BF16)


def _attn_prompt_kernel(lq_ref, g_ref, q_ref, kt_ref, v_ref, o_ref, *scratch,
                        tq, lam_init, pair, running_max):
    qi = pl.program_id(2)
    half = tq // 2
    state = (scratch[0::2], scratch[1::2])

    def block(off, width, r0, diag_col0):
        rows = tq - r0
        v = v_ref[pl.ds(off, width), :]
        for mp in range(2):
            a_ref, l_ref = state[mp][:2]
            q = q_ref[r0:, mp * HEAD_DIM:(mp + 1) * HEAD_DIM]
            kt = kt_ref[mp * HEAD_DIM:(mp + 1) * HEAD_DIM, pl.ds(off, width)]
            s = jnp.dot(q, kt, preferred_element_type=F32)
            vis = None
            if diag_col0 is not None:
                rc = (lax.broadcasted_iota(jnp.int32, (rows, width), 0) + r0) // CHUNK
                cc = (lax.broadcasted_iota(jnp.int32, (rows, width), 1) + diag_col0) // CHUNK
                vis = cc <= rc
            if running_max:
                m_ref = state[mp][2]
                if vis is not None:
                    s = jnp.where(vis, s, NEG)
                m_old = m_ref[r0:, :]
                m_new = jnp.maximum(m_old, jnp.max(s, axis=-1, keepdims=True))
                alpha = jnp.exp2(m_old - m_new)
                p = jnp.exp2(s - m_new)
                m_ref[r0:, :] = m_new
                l_ref[r0:, :] = alpha * l_ref[r0:, :] + jnp.sum(p, axis=-1, keepdims=True)
                a_ref[r0:, :] = alpha * a_ref[r0:, :] + jnp.dot(
                    p.astype(BF16), v, preferred_element_type=F32)
            else:
                p = jnp.exp2(s)
                if vis is not None:
                    p = jnp.where(vis, p, 0.0)
                l_ref[r0:, :] += jnp.sum(p, axis=-1, keepdims=True)
                a_ref[r0:, :] += jnp.dot(p.astype(BF16), v, preferred_element_type=F32)

    for refs in state:
        refs[0][...] = jnp.zeros(refs[0].shape, F32)
        refs[1][...] = jnp.zeros(refs[1].shape, F32)
        if running_max:
            refs[2][...] = jnp.full(refs[2].shape, NEG, F32)

    def body(j, _):
        off = pl.multiple_of(j * (pair * tq), pair * tq)
        for u in range(pair):
            block(off + u * tq, tq, 0, None)
        return 0

    lax.fori_loop(0, qi // pair, body, 0)
    for u in range(pair - 1):
        @pl.when(qi % pair > u)
        def _(u=u):
            block(pl.multiple_of((qi - 1 - u) * tq, tq), tq, 0, None)

    base = pl.multiple_of(qi * tq, tq)
    block(base, half, 0, 0)
    block(base + half, half, half, half)
    (a1, l1), (a2, l2) = (r[:2] for r in state)
    o_ref[...] = _attn_finish(_lambda(lq_ref, lam_init), g_ref[...], a1[...], l1[...],
                              a2[...], l2[...])


def _attend_prompt(q_bf, kt_bf, v_bf, lq, g_eff, lam_init, batch, seq, tq, pair, running_max):
    qk = q_bf.shape[1]
    n_heads = qk // V_DIM
    nq = seq // tq
    kern = functools.partial(_attn_prompt_kernel, tq=tq, lam_init=lam_init, pair=pair,
                             running_max=running_max)
    per_map = [pltpu.VMEM((tq, V_DIM), F32), pltpu.VMEM((tq, 1), F32)]
    if running_max:
        per_map.append(pltpu.VMEM((tq, 1), F32))
    scratch = [s for s in per_map for _ in range(2)]
    return pl.pallas_call(
        kern,
        grid=(batch, n_heads, nq),
        in_specs=[
            pl.BlockSpec((4, HEAD_DIM), lambda b, h, i: (0, 0)),
            pl.BlockSpec((1, V_DIM), lambda b, h, i: (0, 0)),
            pl.BlockSpec((tq, V_DIM), lambda b, h, i: (b * nq + i, h)),
            pl.BlockSpec((V_DIM, seq), lambda b, h, i: (h, b)),
            pl.BlockSpec((seq, V_DIM), lambda b, h, i: (b, h)),
        ],
        out_specs=pl.BlockSpec((tq, V_DIM), lambda b, h, i: (b * nq + i, h)),
        out_shape=jax.ShapeDtypeStruct((batch * seq, qk), BF16),
        scratch_shapes=scratch,
        compiler_params=_params("parallel", "parallel", "arbitrary"),
        name="attn_prompt_max" if running_max else "attn_prompt",
    )(lq, g_eff, q_bf, kt_bf, v_bf)


def _attn_sample_kernel(lq_ref, g_ref, q_ref, kc_ref, vc_ref, kn_ref, vn_ref, o_ref, *,
                        lam_init):
    q = q_ref[...]
    kc = kc_ref[...].astype(BF16)
    vc = vc_ref[...].astype(BF16)
    kn = kn_ref[...].astype(BF16)
    vn = vn_ref[...].astype(BF16)
    res = []
    for mp in range(2):
        sl = slice(mp * HEAD_DIM, (mp + 1) * HEAD_DIM)
        sc = lax.dot_general(q[:, sl], kc[:, sl], (((1,), (1,)), ((), ())),
                             preferred_element_type=F32)
        sn = lax.dot_general(q[:, sl], kn[:, sl], (((1,), (1,)), ((), ())),
                             preferred_element_type=F32)
        m = jnp.maximum(jnp.max(sc, axis=-1, keepdims=True),
                        jnp.max(sn, axis=-1, keepdims=True))
        pc = jnp.exp2(sc - m)
        pn = jnp.exp2(sn - m)
        l = jnp.sum(pc, axis=-1, keepdims=True) + jnp.sum(pn, axis=-1, keepdims=True)
        acc = (jnp.dot(pc.astype(BF16), vc, preferred_element_type=F32)
               + jnp.dot(pn.astype(BF16), vn, preferred_element_type=F32))
        res += [acc, l]
    o_ref[...] = _attn_finish(_lambda(lq_ref, lam_init), g_ref[...], *res)


def _attend_sample(q_bf, k_s, v_s, cache_k2, cache_v2, layer, lq, g_eff, lam_init,
                   row0, dec_batch, dec_seq, past):
    qk = q_bf.shape[1]
    n_heads = qk // V_DIM
    rb0 = row0 // dec_seq
    q_rows = pl.BlockSpec((dec_seq, V_DIM), lambda b, h: (rb0 + b, h))
    new_rows = pl.BlockSpec((dec_seq, V_DIM), lambda b, h: (b, h))
    cache_rows = pl.BlockSpec((past, V_DIM), lambda b, h: (layer * dec_batch + b, h))
    kern = functools.partial(_attn_sample_kernel, lam_init=lam_init)
    return pl.pallas_call(
        kern,
        grid=(dec_batch, n_heads),
        in_specs=[
            pl.BlockSpec((4, HEAD_DIM), lambda b, h: (0, 0)),
            pl.BlockSpec((1, V_DIM), lambda b, h: (0, 0)),
            q_rows, cache_rows, cache_rows, new_rows, new_rows,
        ],
        out_specs=pl.BlockSpec((dec_seq, V_DIM), lambda b, h: (b, h)),
        out_shape=jax.ShapeDtypeStruct((dec_batch * dec_seq, qk), BF16),
        compiler_params=_params("parallel", "parallel"),
        name="attn_sample",
    )(lq, g_eff, q_bf, cache_k2, cache_v2, k_s, v_s)


def _gmlp_kernel(w_ref, b_ref, u_ref, v_ref, o_ref, *, c, n_sub, groups):
    gd = u_ref.shape[1] // groups
    tri = (lax.broadcasted_iota(jnp.int32, (c, c), 0)
           >= lax.broadcasted_iota(jnp.int32, (c, c), 1))
    for g in range(groups):
        wg = jnp.where(tri, w_ref[g], 0.0).astype(BF16)
        bg = b_ref[:, g:g + 1]
        cs = slice(g * gd, (g + 1) * gd)
        for s in range(n_sub):
            rs = slice(s * c, (s + 1) * c)
            mixed = jnp.dot(wg, v_ref[rs, cs].astype(BF16), preferred_element_type=F32) + bg
            o_ref[rs, cs] = (u_ref[rs, cs].astype(F32) * mixed).astype(BF16)


def _gmlp_mix(name, u, vg, w_s, b_s_t, row0, rows, c, n_sub):
    groups = w_s.shape[0]
    gw = u.shape[1]
    tr = c * n_sub
    rb0 = row0 // tr
    rows_spec = pl.BlockSpec((tr, gw), lambda i: (rb0 + i, 0))
    kern = functools.partial(_gmlp_kernel, c=c, n_sub=n_sub, groups=groups)
    return pl.pallas_call(
        kern,
        grid=(rows // tr,),
        in_specs=[
            pl.BlockSpec((groups, c, c), lambda i: (0, 0, 0)),
            pl.BlockSpec((c, groups), lambda i: (0, 0)),
            rows_spec, rows_spec,
        ],
        out_specs=pl.BlockSpec((tr, gw), lambda i: (i, 0)),
        out_shape=jax.ShapeDtypeStruct((rows, gw), BF16),
        compiler_params=_params("parallel"),
        name=name,
    )(w_s, b_s_t, u, vg)


def _merge_kernel(a_ref, m_ref, wa_ref, wb_ref, g0_ref, g1_ref, o_ref):
    ya = jnp.dot(a_ref[...], wa_ref[...], preferred_element_type=F32)
    yb = jnp.dot(m_ref[...], wb_ref[...], preferred_element_type=F32)
    o_ref[...] = (g0_ref[...].astype(F32) * ya + g1_ref[...].astype(F32) * yb).astype(BF16)


def _merge(att, gm, w_pa, w_pb, layer, gates, tm, tn):
    t, wa = att.shape
    d = w_pa.shape[1]
    nj = d // tn
    return pl.pallas_call(
        _merge_kernel,
        grid=(t // tm, nj),
        in_specs=[
            pl.BlockSpec((tm, wa), lambda i, j: (i, 0)),
            pl.BlockSpec((tm, gm.shape[1]), lambda i, j: (i, 0)),
            pl.BlockSpec((wa, tn), lambda i, j: (layer, j)),
            pl.BlockSpec((gm.shape[1], tn), lambda i, j: (layer, j)),
            pl.BlockSpec((tm, tn), lambda i, j: (i, j)),
            pl.BlockSpec((tm, tn), lambda i, j: (i, nj + j)),
        ],
        out_specs=pl.BlockSpec((tm, tn), lambda i, j: (i, j)),
        out_shape=jax.ShapeDtypeStruct((t, d), BF16),
        compiler_params=_params("parallel", "arbitrary"),
        name="merge",
    )(att, gm, w_pa, w_pb, gates, gates)


def _resid_matmul_kernel(x_ref, m_ref, w_ref, o_ref):
    o_ref[...] = x_ref[...] + jnp.dot(m_ref[...], w_ref[...], preferred_element_type=F32)


def _resid_matmul(x, m, w, layer, tm, tn):
    t, d = x.shape
    return pl.pallas_call(
        _resid_matmul_kernel,
        grid=(t // tm, d // tn),
        in_specs=[
            pl.BlockSpec((tm, tn), lambda i, j: (i, j)),
            pl.BlockSpec((tm, m.shape[1]), lambda i, j: (i, 0)),
            pl.BlockSpec((m.shape[1], tn), lambda i, j: (layer, j)),
        ],
        out_specs=pl.BlockSpec((tm, tn), lambda i, j: (i, j)),
        out_shape=jax.ShapeDtypeStruct((t, d), F32),
        compiler_params=_params("parallel", "arbitrary"),
        name="out_proj",
    )(x, m, w)


def _router_kernel(x_ref, g_ref, wr_ref, br_ref, idx_ref, wgt_ref, hp_ref, cnt_ref, run_ref):
    tm, d = x_ref.shape

    @pl.when(pl.program_id(0) == 0)
    def _():
        run_ref[...] = jnp.zeros(run_ref.shape, F32)

    h = _rms(x_ref[...], g_ref[...])
    logits = jnp.dot(h, wr_ref[...], preferred_element_type=F32,
                     precision=lax.Precision.HIGHEST) + br_ref[...]
    n_exp = logits.shape[1]
    lane_e = lax.broadcasted_iota(jnp.int32, logits.shape, 1)
    lane_o = lax.broadcasted_iota(jnp.int32, idx_ref.shape, 1)
    idx_out = jnp.zeros(idx_ref.shape, jnp.int32)
    val_out = jnp.full(wgt_ref.shape, -jnp.inf, F32)
    top = None
    chosen = []
    for k in range(TOP_K):
        m = jnp.max(logits, axis=-1, keepdims=True)
        i = jnp.min(jnp.where(logits == m, lane_e, n_exp), axis=-1, keepdims=True)
        pick = lane_e == i
        chosen.append(pick)
        logits = jnp.where(pick, -jnp.inf, logits)
        idx_out = jnp.where(lane_o == k, i, idx_out)
        val_out = jnp.where(lane_o == k, m, val_out)
        if k == 0:
            top = m
    e = jnp.exp(val_out - top)
    wgt_ref[...] = e / jnp.sum(e, axis=-1, keepdims=True)

    multi = sum(jnp.where(p, 1.0, 0.0) for p in chosen)
    earlier = (lax.broadcasted_iota(jnp.int32, (tm, tm), 1)
               < lax.broadcasted_iota(jnp.int32, (tm, tm), 0))
    before = jnp.dot(jnp.where(earlier, 1.0, 0.0).astype(BF16), multi.astype(BF16),
                     preferred_element_type=F32) + run_ref[...]
    for k in range(TOP_K):
        rank = jnp.sum(jnp.where(chosen[k], before, 0.0), axis=-1, keepdims=True)
        idx_out = jnp.where(lane_o == TOP_K + k, rank.astype(jnp.int32), idx_out)
    idx_ref[...] = idx_out
    run_ref[...] += jnp.sum(multi, axis=0, keepdims=True)
    cnt_ref[...] = run_ref[...]

    nw = d // (2 * LANES)
    bits = lax.bitcast_convert_type(h.astype(BF16).astype(F32), U32)
    for c in range(nw):
        lo = bits[:, c * LANES:(c + 1) * LANES] >> 16
        hi = bits[:, (c + nw) * LANES:(c + nw + 1) * LANES] & jnp.uint32(HI_MASK)
        hp_ref[pl.ds(c, tm, stride=nw), :] = lo | hi


def _router(x, g, w_router, b_router, tm):
    t, d = x.shape
    n_exp = w_router.shape[1]
    nw = d // (2 * LANES)
    return pl.pallas_call(
        _router_kernel,
        grid=(t // tm,),
        in_specs=[
            pl.BlockSpec((tm, d), lambda i: (i, 0)),
            pl.BlockSpec((1, d), lambda i: (0, 0)),
            pl.BlockSpec((d, n_exp), lambda i: (0, 0)),
            pl.BlockSpec((1, n_exp), lambda i: (0, 0)),
        ],
        out_specs=[pl.BlockSpec((tm, LANES), lambda i: (i, 0)),
                   pl.BlockSpec((tm, LANES), lambda i: (i, 0)),
                   pl.BlockSpec((tm * nw, LANES), lambda i: (i, 0)),
                   pl.BlockSpec((1, n_exp), lambda i: (0, 0))],
        out_shape=[jax.ShapeDtypeStruct((t, LANES), jnp.int32),
                   jax.ShapeDtypeStruct((t, LANES), F32),
                   jax.ShapeDtypeStruct((t * nw, LANES), U32),
                   jax.ShapeDtypeStruct((1, n_exp), F32)],
        scratch_shapes=[pltpu.VMEM((1, n_exp), F32)],
        compiler_params=_params("arbitrary"),
        name="router",
    )(x, g.reshape(1, d), w_router, b_router.reshape(1, n_exp))


ISSUE_UNROLL = 8


def _dispatch_kernel(src_ref, hp_hbm, o_ref, buf, sem, *, rows, nw):
    i = pl.program_id(0)
    n = pl.num_programs(0)

    def row_copy(tok_row, slot, r):
        return pltpu.make_async_copy(hp_hbm.at[pl.ds(tok_row, nw), :],
                                     buf.at[slot, pl.ds(r * nw, nw), :], sem.at[slot])

    def issue(tile, slot):
        def body(rb, _):
            for u in range(ISSUE_UNROLL):
                r = rb * ISSUE_UNROLL + u
                tok_row = pl.multiple_of(src_ref[tile * rows + r] * nw, nw)
                row_copy(tok_row, slot, r).start(priority=u % 2)
            return 0
        lax.fori_loop(0, rows // ISSUE_UNROLL, body, 0)

    slot = i % 2

    @pl.when(i == 0)
    def _():
        issue(0, 0)

    @pl.when(i + 1 < n)
    def _():
        issue(i + 1, 1 - slot)

    for r in range(rows):
        row_copy(0, slot, r).wait()

    los, his = [], []
    for c in range(nw):
        w = buf[slot, pl.ds(c, rows, stride=nw), :]
        los.append(lax.bitcast_convert_type(w << 16, F32))
        his.append(lax.bitcast_convert_type(w & jnp.uint32(HI_MASK), F32))
    o_ref[...] = jnp.concatenate(los + his, axis=1).astype(BF16)


def _dispatch(src_tok, hp, d, n_tiles, rows):
    nw = d // (2 * LANES)
    kern = functools.partial(_dispatch_kernel, rows=rows, nw=nw)
    return pl.pallas_call(
        kern,
        grid_spec=pltpu.PrefetchScalarGridSpec(
            num_scalar_prefetch=1,
            grid=(n_tiles,),
            in_specs=[pl.BlockSpec(memory_space=pl.ANY)],
            out_specs=pl.BlockSpec((rows, d), lambda i, s: (i, 0)),
            scratch_shapes=[pltpu.VMEM((2, rows * nw, LANES), U32),
                            pltpu.SemaphoreType.DMA((2,))],
        ),
        out_shape=jax.ShapeDtypeStruct((n_tiles * rows, d), BF16),
        compiler_params=_params("arbitrary"),
        name="dispatch",
    )(src_tok, hp)


def _new_expert(te_ref, i):
    return (i == 0) | (te_ref[i] != te_ref[jnp.maximum(i - 1, 0)])


def _gmm_gu_kernel(te_ref, nt_ref, x_ref, wg_ref, wu_ref, bg_ref, bu_ref, o_ref, wg_bf, wu_bf):
    i = pl.program_id(1)

    @pl.when(_new_expert(te_ref, i))
    def _():
        wg_bf[...] = wg_ref[0].astype(BF16)
        wu_bf[...] = wu_ref[0].astype(BF16)

    @pl.when(i < nt_ref[0])
    def _():
        x = x_ref[...]
        gate = jnp.dot(x, wg_bf[...], preferred_element_type=F32) + bg_ref[0]
        up = jnp.dot(x, wu_bf[...], preferred_element_type=F32) + bu_ref[0]
        gate = jnp.minimum(gate, SWIGLU_LIMIT)
        up = jnp.clip(up, -SWIGLU_LIMIT, SWIGLU_LIMIT)
        act = (up + 1.0) * (gate * jax.nn.sigmoid(SWIGLU_ALPHA * gate))
        o_ref[...] = act.astype(BF16)

    @pl.when(i >= nt_ref[0])
    def _():
        o_ref[...] = jnp.zeros(o_ref.shape, BF16)


def _gmm_gu(tile_exp, n_valid, xs, w_gu, b_gu3, tm, tn):
    p, d = xs.shape
    f = w_gu.shape[2] // 2
    nj = f // tn
    return pl.pallas_call(
        _gmm_gu_kernel,
        grid_spec=pltpu.PrefetchScalarGridSpec(
            num_scalar_prefetch=2,
            grid=(nj, p // tm),
            in_specs=[
                pl.BlockSpec((tm, d), lambda j, i, te, nt: (i, 0)),
                pl.BlockSpec((1, d, tn), lambda j, i, te, nt: (te[i], 0, j)),
                pl.BlockSpec((1, d, tn), lambda j, i, te, nt: (te[i], 0, nj + j)),
                pl.BlockSpec((1, 1, tn), lambda j, i, te, nt: (te[i], 0, j)),
                pl.BlockSpec((1, 1, tn), lambda j, i, te, nt: (te[i], 0, nj + j)),
            ],
            out_specs=pl.BlockSpec((tm, tn), lambda j, i, te, nt: (i, j)),
            scratch_shapes=[pltpu.VMEM((d, tn), BF16), pltpu.VMEM((d, tn), BF16)],
        ),
        out_shape=jax.ShapeDtypeStruct((p, f), BF16),
        compiler_params=_params("arbitrary", "arbitrary"),
        name="expert_gate_up",
    )(tile_exp, n_valid, xs, w_gu, w_gu, b_gu3, b_gu3)


def _gmm_dn_kernel(te_ref, nt_ref, a_ref, w_ref, b_ref, o_ref, w_bf):
    i = pl.program_id(1)

    @pl.when(_new_expert(te_ref, i))
    def _():
        w_bf[...] = w_ref[0].astype(BF16)

    @pl.when(i < nt_ref[0])
    def _():
        o_ref[...] = jnp.dot(a_ref[...], w_bf[...], preferred_element_type=F32) + b_ref[0]

    @pl.when(i >= nt_ref[0])
    def _():
        o_ref[...] = jnp.zeros(o_ref.shape, F32)


def _gmm_dn(tile_exp, n_valid, act, w_dn, b_dn3, tm, tn):
    p, f = act.shape
    d = w_dn.shape[2]
    return pl.pallas_call(
        _gmm_dn_kernel,
        grid_spec=pltpu.PrefetchScalarGridSpec(
            num_scalar_prefetch=2,
            grid=(d // tn, p // tm),
            in_specs=[
                pl.BlockSpec((tm, f), lambda j, i, te, nt: (i, 0)),
                pl.BlockSpec((1, f, tn), lambda j, i, te, nt: (te[i], 0, j)),
                pl.BlockSpec((1, 1, tn), lambda j, i, te, nt: (te[i], 0, j)),
            ],
            out_specs=pl.BlockSpec((tm, tn), lambda j, i, te, nt: (i, j)),
            scratch_shapes=[pltpu.VMEM((f, tn), BF16)],
        ),
        out_shape=jax.ShapeDtypeStruct((p, d), F32),
        compiler_params=_params("arbitrary", "arbitrary"),
        name="expert_down",
    )(tile_exp, n_valid, act, w_dn, b_dn3)


def _combine_kernel(pos_ref, x_ref, w_ref, g_ref, y_hbm, o_ref, h_ref, buf, sem, *, rows):
    i = pl.program_id(0)
    n = pl.num_programs(0)

    def row_copy(src_row, slot, k, r):
        return pltpu.make_async_copy(y_hbm.at[pl.ds(src_row, 1), :],
                                     buf.at[slot, k, pl.ds(r, 1), :], sem.at[slot])

    def issue(tile, slot):
        def body(rb, _):
            for u in range(ISSUE_UNROLL // TOP_K):
                r = rb * (ISSUE_UNROLL // TOP_K) + u
                base = (tile * rows + r) * TOP_K
                for k in range(TOP_K):
                    row_copy(pos_ref[base + k], slot, k, r).start(priority=k % 2)
            return 0
        lax.fori_loop(0, rows * TOP_K // ISSUE_UNROLL, body, 0)

    slot = i % 2

    @pl.when(i == 0)
    def _():
        issue(0, 0)

    @pl.when(i + 1 < n)
    def _():
        issue(i + 1, 1 - slot)

    for r in range(rows):
        for k in range(TOP_K):
            row_copy(0, slot, k, r).wait()

    w = w_ref[...]
    acc = x_ref[...]
    for k in range(TOP_K):
        acc = acc + w[:, k:k + 1] * buf[slot, k]
    o_ref[...] = acc
    h_ref[...] = _rms(acc, g_ref[...]).astype(BF16)


def _combine(pos, x, wgt, g_next, y, rows):
    t, d = x.shape
    kern = functools.partial(_combine_kernel, rows=rows)
    return pl.pallas_call(
        kern,
        grid_spec=pltpu.PrefetchScalarGridSpec(
            num_scalar_prefetch=1,
            grid=(t // rows,),
            in_specs=[
                pl.BlockSpec((rows, d), lambda i, s: (i, 0)),
                pl.BlockSpec((rows, wgt.shape[1]), lambda i, s: (i, 0)),
                pl.BlockSpec((1, d), lambda i, s: (0, 0)),
                pl.BlockSpec(memory_space=pl.ANY),
            ],
            out_specs=[pl.BlockSpec((rows, d), lambda i, s: (i, 0)),
                       pl.BlockSpec((rows, d), lambda i, s: (i, 0))],
            scratch_shapes=[pltpu.VMEM((2, TOP_K, rows, d), F32),
                            pltpu.SemaphoreType.DMA((2,))],
        ),
        out_shape=[jax.ShapeDtypeStruct((t, d), F32), jax.ShapeDtypeStruct((t, d), BF16)],
        compiler_params=_params("arbitrary"),
        name="combine",
    )(pos, x, wgt, g_next.reshape(1, d), y)


def _route_plan(top_i, rank, counts, tm, n_tiles):
    t = top_i.shape[0]
    n_exp = counts.shape[0]
    a = t * TOP_K
    flat_e = top_i.reshape(a)
    rank = rank.reshape(a)
    padded = ((counts + tm - 1) // tm) * tm
    pend = jnp.cumsum(padded)
    pstart = pend - padded
    cstart = jnp.cumsum(counts) - counts
    n_valid = (pend[-1] // tm).astype(jnp.int32)
    pos = (pstart[flat_e] + rank).astype(jnp.int32)

    tile_ids = jnp.arange(n_tiles, dtype=jnp.int32)
    te = jnp.sum((tile_ids * tm)[:, None] >= pend[None, :], axis=1).astype(jnp.int32)
    te = jnp.minimum(te, n_exp - 1)
    te = jnp.where(tile_ids < n_valid, te, te[jnp.maximum(n_valid - 1, 0)])

    order = jnp.argsort(flat_e, stable=True).astype(jnp.int32)
    p = jnp.arange(n_tiles * tm, dtype=jnp.int32)
    e_p = jnp.repeat(te, tm)
    r = p - pstart[e_p]
    valid = (r < counts[e_p]) & (p < pend[-1])
    src_sorted = jnp.clip(cstart[e_p] + r, 0, a - 1)
    src_tok = jnp.where(valid, order[src_sorted] // TOP_K, 0).astype(jnp.int32)
    return src_tok, pos, te, n_valid.reshape(1)


def _moe(x, g, g_next, w_router, b_router, w_gu, b_gu, w_dn, b_dn, layer, cfg):
    t, d = x.shape
    n_exp = w_router.shape[1]
    tm = cfg["moe_tm"]
    n_tiles = (t * TOP_K + n_exp * (tm - 1) + tm - 1) // tm
    idx, wgt, hp, cnt = _router(x, g, w_router, b_router, cfg["router_tm"])
    src_tok, pos, te, n_valid = _route_plan(idx[:, :TOP_K], idx[:, TOP_K:2 * TOP_K],
                                            cnt[0].astype(jnp.int32), tm, n_tiles)
    te = te + layer * n_exp
    xs = _dispatch(src_tok, hp, d, n_tiles, tm)
    act = _gmm_gu(te, n_valid, xs, w_gu, b_gu, tm, cfg["moe_tn"])
    y = _gmm_dn(te, n_valid, act, w_dn, b_dn, tm, cfg["moe_tn"])
    return _combine(pos, x, wgt, g_next, y, cfg["combine_rows"])


def _rope_tables(pos):
    freqs = ROPE_THETA ** (-jnp.arange(0, ROT_DIM, 2, dtype=F32) / ROT_DIM)
    ang = pos.astype(F32)[:, None] * freqs[None, :]
    cos, sin = jnp.cos(ang), jnp.sin(ang)
    t = pos.shape[0]
    rest = HEAD_DIM - ROT_DIM
    cos_tab = jnp.concatenate([cos, cos, jnp.ones((t, rest), F32)], axis=1)
    sin_tab = jnp.concatenate([-sin, sin, jnp.zeros((t, rest), F32)], axis=1)
    return cos_tab, sin_tab


def _mix_matrix():
    src = jnp.arange(HEAD_DIM)[:, None]
    dst = jnp.arange(HEAD_DIM)[None, :]
    perm = ((dst < ROT_HALF) & (src == dst + ROT_HALF)) | (
        (dst >= ROT_HALF) & (dst < ROT_DIM) & (src == dst - ROT_HALF))
    zero = jnp.zeros((HEAD_DIM, HEAD_DIM), F32)
    top = jnp.concatenate([jnp.ones((HEAD_DIM, HEAD_DIM), F32), zero], axis=1)
    bot = jnp.concatenate([zero, perm.astype(F32)], axis=1)
    return jnp.concatenate([top, bot], axis=0).astype(BF16)


def _forward(x_prompt, x_sample, cache_k, cache_v, norm1_g, norm2_g, w_in, q_norm_g, k_norm_g,
             lambda_qk, subln_g, gmlp_ln_g, gmlp_ln_b, gmlp_w_s, gmlp_b_s, gate_b, w_proj_a,
             w_proj_b, w_out, w_router, b_router, w_gu, b_gu, w_dn, b_dn):
    batch, seq, d = x_prompt.shape
    dec_batch, dec_seq, _ = x_sample.shape
    depth, _, past = cache_k.shape[:3]
    gw = gmlp_ln_g.shape[1]
    qk = (w_in.shape[2] - 2 * gw - 2 * d) // 3
    n_heads = qk // V_DIM
    tp, ts = batch * seq, dec_batch * dec_seq
    t_all = tp + ts
    assert tp % ts == 0 and seq % GMLP_CHUNK == 0 and past % GMLP_CHUNK == 0

    cfg = dict(
        tm=_pick(t_all, (1536, 512, 256, 128)),
        tm_p=_pick(tp, (1024, 512, 256, 128)),
        tm_s=_pick(ts, (512, 256, 128, 64)),
        tn=_pick(qk, (512, 256)),
        tn_d=_pick(d, (512, 256)),
        tq=_pick(seq, (1024, 512, 256, 128)),
        attn_pair=2,
        router_tm=_pick(t_all, (256, 128)),
        moe_tm=256 if t_all >= 4096 else 64,
        moe_tn=_pick(d, (1024, 512, 256)),
        combine_rows=64,
        gmlp_sub=_pick(seq // GMLP_CHUNK, (4, 2, 1)),
    )
    tm, tn = cfg["tm"], cfg["tn"]

    pos_all = jnp.concatenate([jnp.tile(jnp.arange(seq), batch),
                               jnp.tile(past + jnp.arange(dec_seq), dec_batch)])
    rope = _rope_tables(pos_all)
    x = jnp.concatenate([x_prompt.reshape(tp, d), x_sample.reshape(ts, d)], axis=0)
    cache_k2 = cache_k.reshape(depth * dec_batch * past, qk)
    cache_v2 = cache_v.reshape(depth * dec_batch * past, qk)

    h = _prenorm(x, norm1_g[0], _pick(t_all, (512, 256, 128)))
    tiles = qk // LANES
    kout = jnp.zeros((depth * tp * tiles, LANES), F32)
    vout = jnp.zeros((depth * tp * tiles, LANES), F32)
    n_exp = w_router.shape[2]
    w_in_bf = w_in.astype(BF16).reshape(depth * d, -1)
    w_pa_bf = w_proj_a.astype(BF16).reshape(-1, d)
    w_pb_bf = w_proj_b.astype(BF16).reshape(-1, d)
    w_out_bf = w_out.astype(BF16).reshape(depth * d, d)
    w_gu3 = w_gu.reshape(depth * n_exp, d, -1)
    b_gu3 = b_gu.reshape(depth * n_exp, 1, -1)
    w_dn3 = w_dn.reshape(depth * n_exp, -1, d)
    b_dn3 = b_dn.reshape(depth * n_exp, 1, d)
    ks, vs, gs = [], [], []
    for l in range(depth):
        lam_init = 0.8 - 0.6 * math.exp(-0.3 * l)
        g_eff = (subln_g[l] * (1.0 - lam_init)).reshape(1, V_DIM)
        q_bf, kt_bf, v_bf, k_s, v_s, u_act, vg_n, gates, kout, vout = _project_in(
            h, w_in_bf, q_norm_g[l], k_norm_g[l], rope,
            gmlp_ln_g[l], gmlp_ln_b[l], gate_b[l], (qk, gw), cfg, l, tp, kout, vout)

        score_bound = (HEAD_DIM * Q_SCALE * 1.01 * jnp.max(jnp.abs(q_norm_g[l]))
                       * jnp.max(jnp.abs(k_norm_g[l])))
        attend = functools.partial(_attend_prompt, q_bf, kt_bf, v_bf, lambda_qk[l], g_eff,
                                   lam_init, batch, seq, cfg["tq"], cfg["attn_pair"])
        o_p = lax.cond(score_bound <= SCORE_BOUND_LIMIT,
                       lambda: attend(False), lambda: attend(True))
        o_s = _attend_sample(q_bf, k_s, v_s, cache_k2, cache_v2, l, lambda_qk[l], g_eff,
                             lam_init, tp, dec_batch, dec_seq, past)
        att = jnp.concatenate([o_p, o_s], axis=0)

        gm_p = _gmlp_mix("gmlp_prompt", u_act, vg_n, gmlp_w_s[l], gmlp_b_s[l].T, 0, tp,
                         GMLP_CHUNK, cfg["gmlp_sub"])
        gm_s = _gmlp_mix("gmlp_sample", u_act, vg_n, gmlp_w_s[l][:, :dec_seq, :dec_seq],
                         gmlp_b_s[l][:, :dec_seq].T, tp, ts, dec_seq, dec_batch)
        gm = jnp.concatenate([gm_p, gm_s], axis=0)

        merged = _merge(att, gm, w_pa_bf, w_pb_bf, l, gates, tm, cfg["tn_d"])
        x = _resid_matmul(x, merged, w_out_bf, l, tm, cfg["tn_d"])

        g_next = norm1_g[min(l + 1, depth - 1)]
        x, h = _moe(x, norm2_g[l], g_next, w_router[l], b_router[l], w_gu3, b_gu3, w_dn3,
                    b_dn3, l, cfg)

        ks.append(k_s.reshape(dec_batch, dec_seq, n_heads, 2, HEAD_DIM))
        vs.append(v_s.reshape(dec_batch, dec_seq, n_heads, V_DIM))
        gs.append(vg_n[tp:].reshape(dec_batch, dec_seq, gw))

    per_head = V_DIM // LANES
    new_k_prompt = kout.reshape(depth, batch, seq, n_heads, 2, HEAD_DIM)
    new_v_prompt = (vout.reshape(depth, batch, seq, per_head, n_heads, LANES)
                    .transpose(0, 1, 2, 4, 3, 5).reshape(depth, batch, seq, n_heads, V_DIM))
    return (x[:tp].reshape(batch, seq, d), x[tp:].reshape(dec_batch, dec_seq, d),
            new_k_prompt, new_v_prompt, jnp.stack(ks), jnp.stack(vs), jnp.stack(gs))


def kernel(x_prompt, x_sample, cache_k, cache_v, norm1_g, norm2_g, w_in, q_norm_g, k_norm_g, lambda_qk, subln_g, gmlp_ln_g, gmlp_ln_b, gmlp_w_s, gmlp_b_s, gate_b, w_proj_a, w_proj_b, w_out, w_router, b_router, w_gu, b_gu, w_dn, b_dn):
    return _forward(x_prompt, x_sample, cache_k, cache_v, norm1_g, norm2_g, w_in, q_norm_g,
                    k_norm_g, lambda_qk, subln_g, gmlp_ln_g, gmlp_ln_b, gmlp_w_s, gmlp_b_s,
                    gate_b, w_proj_a, w_proj_b, w_out, w_router, b_router, w_gu, b_gu, w_dn,
                    b_dn)
```
